```python
import jax, jax.numpy as jnp
from jax import lax
import numpy as np

D_MODEL = 1024
BATCH = 4
SEQ = 8192
DEPTH = 2

A_WIDTH = D_MODEL // 2
A_HEADS = 8
CONV_WIDTH = 3
B_WIDTH = D_MODEL // 2
POOL_WINDOWS = (2, 4, 8, 16)
B_GROUPS = len(POOL_WINDOWS)
B_GROUP_DIM = B_WIDTH // B_GROUPS
EVEN_IN = 4 * A_WIDTH + 2 * B_WIDTH
EVEN_MIX = A_WIDTH + B_WIDTH
C_WIDTH = D_MODEL
C_HEADS = 8
C_HEAD_DIM = C_WIDTH // C_HEADS
CHUNK = 128
ODD_IN = 3 * C_WIDTH
N_EVEN = (DEPTH + 1) // 2
N_ODD = DEPTH // 2
EPS = 1e-6

kernel_name = "hybrid_conv_pool_gmlp_trunk"


def rmsnorm(x, g):
    xf = x.astype(jnp.float32)
    y = xf * lax.rsqrt(jnp.mean(xf * xf, axis=-1, keepdims=True) + EPS)
    return (y * g.astype(jnp.float32)).astype(x.dtype)


def layernorm(x, g, b):
    xf = x.astype(jnp.float32)
    mu = jnp.mean(xf, axis=-1, keepdims=True)
    var = jnp.mean(jnp.square(xf - mu), axis=-1, keepdims=True)
    y = (xf - mu) * lax.rsqrt(var + EPS)
    return (y * g.astype(jnp.float32) + b.astype(jnp.float32)).astype(x.dtype)


def short_gated_conv(xa, gb, gc, conv_w):
    S = xa.shape[1]
    h = gc * xa
    hp = jnp.pad(h, ((0, 0), (CONV_WIDTH - 1, 0), (0, 0)))
    conv = sum(conv_w[k] * hp[:, k:k + S] for k in range(CONV_WIDTH))
    return gb * conv


def multiscale_pool(xp, pool_w, pool_scale):
    Bsz, S, _ = xp.shape
    xf = xp.astype(jnp.float32)
    cs = jnp.cumsum(xf, axis=1)
    pos = jnp.arange(S)
    outs = []
    for g, w in enumerate(POOL_WINDOWS):
        sl = slice(g * B_GROUP_DIM, (g + 1) * B_GROUP_DIM)
        cs_g = cs[..., sl]
        lower = jnp.pad(cs_g, ((0, 0), (w, 0), (0, 0)))[:, :S]
        count = jnp.minimum(pos + 1, w).astype(jnp.float32)[None, :, None]
        outs.append((cs_g - lower) / count - xf[..., sl])
    pooled = jnp.stack(outs, axis=2).astype(xp.dtype)
    mixed = jnp.einsum('bsgc,gcd->bsgd', pooled, pool_w)
    return mixed.reshape(Bsz, S, B_WIDTH) * pool_scale


def even_layer(h, w_in, conv_w, pool_w, pool_scale, w_out):
    proj = h @ w_in
    xa, gb, gc, za, xp, zp = jnp.split(
        proj, np.cumsum([A_WIDTH] * 4 + [B_WIDTH]).tolist(), axis=-1)
    ya = short_gated_conv(xa, gb, gc, conv_w) * jax.nn.silu(za)
    yb = multiscale_pool(xp, pool_w, pool_scale) * jax.nn.silu(zp)
    return jnp.concatenate([ya, yb], axis=-1) @ w_out


def odd_layer(h, w_in, ln_g, ln_b, w_s, b_s, w_out):
    Bsz, S, _ = h.shape
    proj = h @ w_in
    u, v, z = jnp.split(proj, 3, axis=-1)
    v = layernorm(v, ln_g, ln_b)
    vc = v.reshape(Bsz, S // CHUNK, CHUNK, C_HEADS, C_HEAD_DIM)
    ws = jnp.tril(w_s)
    sv = jnp.einsum('hts,bnshc->bnthc', ws, vc) + b_s.T[None, None, :, :, None]
    y = u * sv.reshape(Bsz, S, C_WIDTH) * jax.nn.silu(z)
    return y @ w_out


def setup_inputs(seed: int = 0) -> dict:
    key = jax.random.key(seed)
    ks = jax.random.split(key, 16)
    f32 = jnp.float32
    nrm = lambda k, shape, s: jax.random.normal(k, shape, f32) * s
    return {
        "x": nrm(ks[0], (BATCH, SEQ, D_MODEL), 1.0),
        "pre_norm": 1.0 + nrm(ks[1], (DEPTH, D_MODEL), 0.05),
        "post_norm": 1.0 + nrm(ks[2], (DEPTH, D_MODEL), 0.05),
        "even_w_in": nrm(ks[3], (N_EVEN, D_MODEL, EVEN_IN), D_MODEL ** -0.5),
        "even_conv_w": nrm(ks[4], (N_EVEN, CONV_WIDTH, A_WIDTH), CONV_WIDTH ** -0.5),
        "even_pool_w": nrm(ks[5], (N_EVEN, B_GROUPS, B_GROUP_DIM, B_GROUP_DIM), B_GROUP_DIM ** -0.5),
        "even_pool_scale": 1.0 + nrm(ks[6], (N_EVEN, B_WIDTH), 0.1),
        "even_w_out": nrm(ks[7], (N_EVEN, EVEN_MIX, D_MODEL), EVEN_MIX ** -0.5),
        "odd_w_in": nrm(ks[8], (N_ODD, D_MODEL, ODD_IN), D_MODEL ** -0.5),
        "odd_ln_g": 1.0 + nrm(ks[9], (N_ODD, C_WIDTH), 0.05),
        "odd_ln_b": nrm(ks[10], (N_ODD, C_WIDTH), 0.02),
        "odd_w_s": nrm(ks[11], (N_ODD, C_HEADS, CHUNK, CHUNK), CHUNK ** -0.5),
        "odd_b_s": 1.0 + nrm(ks[12], (N_ODD, C_HEADS, CHUNK), 0.1),
        "odd_w_out": nrm(ks[13], (N_ODD, C_WIDTH, D_MODEL), C_WIDTH ** -0.5),
    }


def reference(x, pre_norm, post_norm, even_w_in, even_conv_w, even_pool_w,
              even_pool_scale, even_w_out, odd_w_in, odd_ln_g, odd_ln_b,
              odd_w_s, odd_b_s, odd_w_out):
    for i in range(DEPTH):
        h = rmsnorm(x, pre_norm[i])
        j = i // 2
        if i % 2 == 0:
            m = even_layer(h, even_w_in[j], even_conv_w[j], even_pool_w[j],
                           even_pool_scale[j], even_w_out[j])
        else:
            m = odd_layer(h, odd_w_in[j], odd_ln_g[j], odd_ln_b[j],
                          odd_w_s[j], odd_b_s[j], odd_w_out[j])
        x = x + rmsnorm(m, post_norm[i])
    return x
```

```python
import functools

import jax
import jax.numpy as jnp
from jax import lax
from jax.experimental import pallas as pl
from jax.experimental.pallas import tpu as pltpu

D_MODEL = 1024
A_WIDTH = 512
B_WIDTH = 512
CONV_WIDTH = 3
POOL_WINDOWS = (2, 4, 8, 16)
GROUP_DIM = B_WIDTH // len(POOL_WINDOWS)
C_HEADS = 8
C_HEAD_DIM = 128
CHUNK = 128
EPS = 1e-6

ROW_TILE = 512
CONV_HALO = 8
POOL_HALO = 16
VMEM_LIMIT_BYTES = 56 * 1024 * 1024

_F32 = jnp.float32
_BF16 = jnp.bfloat16


def _rms_scale(v, gain):
    ms = jnp.mean(v * v, axis=-1, keepdims=True)
    return v * lax.rsqrt(ms + EPS) * gain


def _silu(v):
    return v * jax.nn.sigmoid(v)


def _even_kernel(x_ref, pre_ref, post_ref, win_ref, convw_ref, poolw_ref,
                 pscale_ref, wout_ref, o_ref, hc_carry, xp_carry):
    tm = x_ref.shape[1]
    t = pl.program_id(1)

    @pl.when(t == 0)
    def _():
        hc_carry[...] = jnp.zeros_like(hc_carry)
        xp_carry[...] = jnp.zeros_like(xp_carry)

    x = x_ref[0]
    h = _rms_scale(x, pre_ref[...]).astype(_BF16)
    proj = jnp.dot(h, win_ref[...], preferred_element_type=_F32)
    xa = proj[:, 0 * A_WIDTH:1 * A_WIDTH]
    gb = proj[:, 1 * A_WIDTH:2 * A_WIDTH]
    gc = proj[:, 2 * A_WIDTH:3 * A_WIDTH]
    za = proj[:, 3 * A_WIDTH:4 * A_WIDTH]
    xp = proj[:, 4 * A_WIDTH:4 * A_WIDTH + B_WIDTH]
    zp = proj[:, 4 * A_WIDTH + B_WIDTH:]

    hc = gc * xa
    ext = jnp.concatenate([hc_carry[...], hc], axis=0)
    hc_carry[...] = hc[tm - CONV_HALO:, :]
    hc1 = pltpu.roll(ext, 1, axis=0)[CONV_HALO:, :]
    hc2 = pltpu.roll(ext, 2, axis=0)[CONV_HALO:, :]
    cw = convw_ref[...]
    conv = cw[0:1, :] * hc2 + cw[1:2, :] * hc1 + cw[2:3, :] * hc
    ya = gb * conv * _silu(za)

    extp = jnp.concatenate([xp_carry[...], xp], axis=0)
    xp_carry[...] = xp[tm - POOL_HALO:, :]
    sums = []
    acc = extp
    shift = 1
    for g in range(len(POOL_WINDOWS)):
        acc = acc + pltpu.roll(acc, shift, axis=0)
        sums.append(acc[POOL_HALO:, :GROUP_DIM])
        acc = acc[:, GROUP_DIM:]
        shift *= 2
    pos1 = t * tm + lax.broadcasted_iota(jnp.int32, (tm, GROUP_DIM), 0) + 1
    mixed = []
    for g, w in enumerate(POOL_WINDOWS):
        count = jnp.minimum(pos1, w).astype(_F32)
        pooled = sums[g] / count - xp[:, g * GROUP_DIM:(g + 1) * GROUP_DIM]
        mixed.append(jnp.dot(pooled.astype(_BF16), poolw_ref[g],
                             preferred_element_type=_F32))
    yb = jnp.concatenate(mixed, axis=-1) * pscale_ref[...] * _silu(zp)

    mix = jnp.concatenate([ya, yb], axis=-1).astype(_BF16)
    m = jnp.dot(mix, wout_ref[...], preferred_element_type=_F32)
    o_ref[0] = x + _rms_scale(m, post_ref[...])


def _odd_kernel(x_ref, pre_ref, post_ref, win_ref, lng_ref, lnb_ref, ws_ref,
                bias_ref, wout_ref, o_ref):
    tm = x_ref.shape[1]
    n_chunks = tm // CHUNK
    x = x_ref[0]
    h = _rms_scale(x, pre_ref[...]).astype(_BF16)
    proj = jnp.dot(h, win_ref[...], preferred_element_type=_F32)
    u = proj[:, :D_MODEL]
    v = proj[:, D_MODEL:2 * D_MODEL]
    z = proj[:, 2 * D_MODEL:]

    mu = jnp.mean(v, axis=-1, keepdims=True)
    vc = v - mu
    var = jnp.mean(vc * vc, axis=-1, keepdims=True)
    vn = (vc * lax.rsqrt(var + EPS) * lng_ref[...] + lnb_ref[...]).astype(_BF16)

    row = lax.broadcasted_iota(jnp.int32, (CHUNK, CHUNK), 0)
    col = lax.broadcasted_iota(jnp.int32, (CHUNK, CHUNK), 1)
    causal = col <= row
    sv_heads = []
    for hd in range(C_HEADS):
        ws = jnp.where(causal, ws_ref[hd], 0.0).astype(_BF16)
        lanes = slice(hd * C_HEAD_DIM, (hd + 1) * C_HEAD_DIM)
        rhs = jnp.concatenate(
            [vn[n * CHUNK:(n + 1) * CHUNK, lanes] for n in range(n_chunks)],
            axis=-1)
        out = jnp.dot(ws, rhs, preferred_element_type=_F32)
        sv_heads.append(jnp.concatenate(
            [out[:, n * C_HEAD_DIM:(n + 1) * C_HEAD_DIM]
             for n in range(n_chunks)], axis=0))
    sv = jnp.concatenate(sv_heads, axis=-1)
    bias = bias_ref[...]
    sv = sv + jnp.concatenate([bias] * n_chunks, axis=0)

    y = (u * sv * _silu(z)).astype(_BF16)
    m = jnp.dot(y, wout_ref[...], preferred_element_type=_F32)
    o_ref[0] = x + _rms_scale(m, post_ref[...])


def _full(shape):
    return pl.BlockSpec(shape, lambda b, t: (0,) * len(shape))


def _layer_call(body, x, params, scratch_shapes=()):
    bsz, seq, d = x.shape
    row_spec = pl.BlockSpec((1, ROW_TILE, d), lambda b, t: (b, t, 0))
    return pl.pallas_call(
        body,
        grid=(bsz, seq // ROW_TILE),
        in_specs=[row_spec] + [_full(p.shape) for p in params],
        out_specs=row_spec,
        out_shape=jax.ShapeDtypeStruct(x.shape, x.dtype),
        scratch_shapes=list(scratch_shapes),
        compiler_params=pltpu.CompilerParams(
            dimension_semantics=("arbitrary", "arbitrary"),
            vmem_limit_bytes=VMEM_LIMIT_BYTES),
    )(x, *params)


def _even_layer(x, pre, post, w_in, conv_w, pool_w, pool_scale, w_out):
    params = (pre.reshape(1, -1), post.reshape(1, -1), w_in.astype(_BF16),
              conv_w, pool_w.astype(_BF16), pool_scale.reshape(1, -1),
              w_out.astype(_BF16))
    scratch = (pltpu.VMEM((CONV_HALO, A_WIDTH), _F32),
               pltpu.VMEM((POOL_HALO, B_WIDTH), _F32))
    return _layer_call(_even_kernel, x, params, scratch)


def _odd_layer(x, pre, post, w_in, ln_g, ln_b, w_s, b_s, w_out):
    bias = jnp.repeat(b_s.T, C_HEAD_DIM, axis=1)
    params = (pre.reshape(1, -1), post.reshape(1, -1), w_in.astype(_BF16),
              ln_g.reshape(1, -1), ln_b.reshape(1, -1), w_s, bias,
              w_out.astype(_BF16))
    return _layer_call(_odd_kernel, x, params)


def kernel(x, pre_norm, post_norm, even_w_in, even_conv_w, even_pool_w, even_pool_scale, even_w_out, odd_w_in, odd_ln_g, odd_ln_b, odd_w_s, odd_b_s, odd_w_out):
    depth = pre_norm.shape[0]
    for i in range(depth):
        j = i // 2
        if i % 2 == 0:
            x = _even_layer(x, pre_norm[i], post_norm[i], even_w_in[j],
                            even_conv_w[j], even_pool_w[j], even_pool_scale[j],
                            even_w_out[j])
        else:
            x = _odd_layer(x, pre_norm[i], post_norm[i], odd_w_in[j],
                           odd_ln_g[j], odd_ln_b[j], odd_w_s[j], odd_b_s[j],
                           odd_w_out[j])
    return x
```

```python
import functools

import jax
import jax.numpy as jnp
from jax import lax
from jax.experimental import pallas as pl
from jax.experimental.pallas import tpu as pltpu

D_MODEL = 1024
A_WIDTH = 512
B_WIDTH = 512
POOL_WINDOWS = (2, 4, 8, 16)
GROUP_DIM = B_WIDTH // len(POOL_WINDOWS)
C_HEADS = 8
C_HEAD_DIM = 128
CHUNK = 128
EPS = 1e-6

ROW_TILE = 1024
POST_ROWS = 256
SLAB = 256
CONV_HALO = 8
POOL_HALO = 16
VMEM_LIMIT_BYTES = 56 * 1024 * 1024

_F32 = jnp.float32
_BF16 = jnp.bfloat16


def _silu(v):
    return v * jax.nn.sigmoid(v)


def _dot(a, b):
    return jnp.dot(a, b, preferred_element_type=_F32)


def _row_scale(v):
    return lax.rsqrt(jnp.mean(v * v, axis=-1, keepdims=True) + EPS)


def _even_mix(xg, r, mix_ref, t, win_ref, convw_ref, poolw_ref, pscale_ref,
              hc_carry, xp_carry):
    tm = xg.shape[0]
    first = t == 0
    groups_per_slab = SLAB // GROUP_DIM
    r2 = r * r

    def proj(col):
        return _dot(xg, win_ref[:, col:col + SLAB])

    for j in range(A_WIDTH // SLAB):
        lanes = slice(j * SLAB, (j + 1) * SLAB)
        xa, gb, gc, za = [proj(k * A_WIDTH + j * SLAB) for k in range(4)]
        cw = convw_ref[:, lanes]
        hc = gc * xa * r2
        carry = jnp.where(first, 0.0, hc_carry[:, lanes])
        hc_carry[:, lanes] = hc[tm - CONV_HALO:, :]
        ext = jnp.concatenate([carry, hc], axis=0)
        hc1 = pltpu.roll(ext, 1, axis=0)[CONV_HALO:, :]
        hc2 = pltpu.roll(ext, 2, axis=0)[CONV_HALO:, :]
        conv = cw[0:1] * hc2 + cw[1:2] * hc1 + cw[2:3] * hc
        mix_ref[:, lanes] = (gb * r * conv * _silu(za * r)).astype(_BF16)

    pos1 = t * tm + lax.broadcasted_iota(jnp.int32, (tm, GROUP_DIM), 0) + 1
    inv_pos = 1.0 / pos1.astype(_F32)
    for j in range(B_WIDTH // SLAB):
        lanes = slice(j * SLAB, (j + 1) * SLAB)
        xp, zp = [proj(4 * A_WIDTH + k * B_WIDTH + j * SLAB) for k in range(2)]
        xp = xp * r
        carry = jnp.where(first, 0.0, xp_carry[:, lanes])
        xp_carry[:, lanes] = xp[tm - POOL_HALO:, :]
        acc = jnp.concatenate([carry, xp], axis=0)
        shift = 1
        for _ in range(j * groups_per_slab):
            acc = acc + pltpu.roll(acc, shift, axis=0)
            shift *= 2
        pooled = []
        for k in range(groups_per_slab):
            w = POOL_WINDOWS[j * groups_per_slab + k]
            acc = acc + pltpu.roll(acc, shift, axis=0)
            shift *= 2
            inv_count = jnp.where(pos1 < w, inv_pos, 1.0 / w)
            pooled.append(acc[POOL_HALO:, :GROUP_DIM] * inv_count
                          - xp[:, k * GROUP_DIM:(k + 1) * GROUP_DIM])
            acc = acc[:, GROUP_DIM:]
        pooled = jnp.concatenate(pooled, axis=-1).astype(_BF16)
        mixed = _dot(pooled, poolw_ref[j])
        mix_ref[:, A_WIDTH + j * SLAB:A_WIDTH + (j + 1) * SLAB] = (
            mixed * pscale_ref[:, lanes] * _silu(zp * r)).astype(_BF16)


def _odd_mix(xg, r, mix_ref, t, win_ref, lng_ref, lnb_ref, ws_ref, bias_ref):
    del t
    tm = xg.shape[0]
    n_chunks = tm // CHUNK
    n_slabs = D_MODEL // SLAB
    heads_per_slab = SLAB // C_HEAD_DIM

    def proj(col):
        return _dot(xg, win_ref[:, col:col + SLAB])

    pv = [proj(D_MODEL + j * SLAB) for j in range(n_slabs)]
    mu = jnp.sum((pv[0] + pv[1]) + (pv[2] + pv[3]), axis=-1,
                 keepdims=True) * (1.0 / D_MODEL)
    vc = [p - mu for p in pv]
    sq = [c * c for c in vc]
    var = jnp.sum((sq[0] + sq[1]) + (sq[2] + sq[3]), axis=-1,
                  keepdims=True) * (1.0 / D_MODEL)
    rstd = r * lax.rsqrt(r * r * var + EPS)

    row = lax.broadcasted_iota(jnp.int32, (CHUNK, CHUNK), 0)
    col = lax.broadcasted_iota(jnp.int32, (CHUNK, CHUNK), 1)
    for j in range(n_slabs):
        lanes = slice(j * SLAB, (j + 1) * SLAB)
        vn = (vc[j] * rstd * lng_ref[:, lanes] + lnb_ref[:, lanes]).astype(_BF16)
        sv = []
        for k in range(heads_per_slab):
            hd = j * heads_per_slab + k
            ws = jnp.where(col <= row, ws_ref[hd], 0.0).astype(_BF16)
            hl = slice(k * C_HEAD_DIM, (k + 1) * C_HEAD_DIM)
            rhs = jnp.concatenate(
                [vn[n * CHUNK:(n + 1) * CHUNK, hl] for n in range(n_chunks)],
                axis=-1)
            out = _dot(ws, rhs)
            sv.append(jnp.concatenate(
                [out[:, n * C_HEAD_DIM:(n + 1) * C_HEAD_DIM]
                 for n in range(n_chunks)], axis=0))
        bias = bias_ref[:, lanes]
        sv = (jnp.concatenate(sv, axis=-1)
              + jnp.concatenate([bias] * n_chunks, axis=0))
        u = proj(j * SLAB)
        z = proj(2 * D_MODEL + j * SLAB)
        mix_ref[:, lanes] = (u * r * sv * _silu(z * r)).astype(_BF16)


def _layer_kernel(mix_fn, n_params, *refs):
    x_ref, pre_ref, post_ref, win_ref = refs[:4]
    params = refs[4:4 + n_params]
    wout_ref, o_ref, mix_scr = refs[4 + n_params:7 + n_params]
    carries = refs[7 + n_params:]
    tm = x_ref.shape[1]
    t = pl.program_id(1)

    x = x_ref[0]
    xg = (x * pre_ref[...]).astype(_BF16)
    mix_fn(xg, _row_scale(x), mix_scr, t, win_ref, *params, *carries)

    for c in range(tm // POST_ROWS):
        rows = slice(c * POST_ROWS, (c + 1) * POST_ROWS)
        m = _dot(mix_scr[rows, :], wout_ref[...])
        o_ref[0, rows, :] = (x_ref[0, rows, :]
                             + m * _row_scale(m) * post_ref[...])


def _full(shape):
    return pl.BlockSpec(shape, lambda b, t: (0,) * len(shape),
                        pipeline_mode=pl.Buffered(1))


def _layer_call(name, mix_fn, x, pre, post, w_in, params, w_out, carry_shapes):
    bsz, seq, d = x.shape
    row_spec = pl.BlockSpec((1, ROW_TILE, d), lambda b, t: (b, t, 0))
    operands = (x, pre.reshape(1, -1), post.reshape(1, -1), w_in, *params,
                w_out)
    scratch = [pltpu.VMEM((ROW_TILE, d), _BF16)]
    scratch += [pltpu.VMEM(s, _F32) for s in carry_shapes]
    return pl.pallas_call(
        functools.partial(_layer_kernel, mix_fn, len(params)),
        name=name,
        grid=(bsz, seq // ROW_TILE),
        in_specs=[row_spec] + [_full(p.shape) for p in operands[1:]],
        out_specs=row_spec,
        out_shape=jax.ShapeDtypeStruct(x.shape, x.dtype),
        scratch_shapes=scratch,
        compiler_params=pltpu.CompilerParams(
            dimension_semantics=("arbitrary", "arbitrary"),
            vmem_limit_bytes=VMEM_LIMIT_BYTES),
    )(*operands)


def _even_layer(x, pre, post, w_in, conv_w, pool_w, pool_scale, w_out):
    pw = pool_w.astype(_BF16)
    zero = jnp.zeros_like(pw[0])
    pool_bd = jnp.stack([
        jnp.block([[pw[2 * j], zero], [zero, pw[2 * j + 1]]])
        for j in range(B_WIDTH // SLAB)])
    params = (conv_w, pool_bd, pool_scale.reshape(1, -1))
    carries = ((CONV_HALO, A_WIDTH), (POOL_HALO, B_WIDTH))
    return _layer_call("even_layer", _even_mix, x, pre, post,
                       w_in.astype(_BF16), params, w_out.astype(_BF16), carries)


def _odd_layer(x, pre, post, w_in, ln_g, ln_b, w_s, b_s, w_out):
    bias = jnp.repeat(b_s.T, C_HEAD_DIM, axis=1)
    params = (ln_g.reshape(1, -1), ln_b.reshape(1, -1), w_s, bias)
    return _layer_call("odd_layer", _odd_mix, x, pre, post,
                       w_in.astype(_BF16), params, w_out.astype(_BF16), ())


def kernel(x, pre_norm, post_norm, even_w_in, even_conv_w, even_pool_w, even_pool_scale, even_w_out, odd_w_in, odd_ln_g, odd_ln_b, odd_w_s, odd_b_s, odd_w_out):
    depth = pre_norm.shape[0]
    for i in range(depth):
        j = i // 2
        if i % 2 == 0:
            x = _even_layer(x, pre_norm[i], post_norm[i], even_w_in[j],
                            even_conv_w[j], even_pool_w[j], even_pool_scale[j],
                            even_w_out[j])
        else:
            x = _odd_layer(x, pre_norm[i], post_norm[i], odd_w_in[j],
                           odd_ln_g[j], odd_ln_b[j], odd_w_s[j], odd_b_s[j],
                           odd_w_out[j])
    return x
```

```python
import functools

import jax
import jax.numpy as jnp
from jax import lax
from jax.experimental import pallas as pl
from jax.experimental.pallas import tpu as pltpu

D_MODEL = 1024
A_WIDTH = 512
B_WIDTH = 512
POOL_WINDOWS = (2, 4, 8, 16)
GROUP_DIM = B_WIDTH // len(POOL_WINDOWS)
C_HEADS = 8
C_HEAD_DIM = 128
CHUNK = 128
EPS = 1e-6

ROW_TILE = 1024
POST_ROWS = 512
SLAB = 256
CONV_HALO = 8
POOL_HALO = 16
VMEM_LIMIT_BYTES = 56 * 1024 * 1024

_F32 = jnp.float32
_BF16 = jnp.bfloat16


def _silu(v):
    half = 0.5 * v
    return half + half * jnp.tanh(half)


def _dot(a, b):
    return jnp.dot(a, b, preferred_element_type=_F32)


def _row_scale(v):
    return lax.rsqrt(jnp.mean(v * v, axis=-1, keepdims=True) + EPS)


def _even_mix(xg, r, mix_ref, t, win_ref, convw_ref, poolw_ref, pscale_ref,
              hc_carry, xp_carry):
    tm = xg.shape[0]
    first = t == 0
    groups_per_slab = SLAB // GROUP_DIM
    r2 = r * r

    def proj(col):
        return _dot(xg, win_ref[:, col:col + SLAB])

    for j in range(A_WIDTH // SLAB):
        lanes = slice(j * SLAB, (j + 1) * SLAB)
        xa, gb, gc, za = [proj(k * A_WIDTH + j * SLAB) for k in range(4)]
        cw = convw_ref[:, lanes]
        hc = gc * xa * r2
        carry = jnp.where(first, 0.0, hc_carry[:, lanes])
        hc_carry[:, lanes] = hc[tm - CONV_HALO:, :]
        ext = jnp.concatenate([carry, hc], axis=0)
        hc1 = pltpu.roll(ext, 1, axis=0)[CONV_HALO:, :]
        hc2 = pltpu.roll(ext, 2, axis=0)[CONV_HALO:, :]
        conv = cw[0:1] * hc2 + cw[1:2] * hc1 + cw[2:3] * hc
        mix_ref[:, lanes] = (gb * r * conv * _silu(za * r)).astype(_BF16)

    pos1 = t * tm + lax.broadcasted_iota(jnp.int32, (tm, GROUP_DIM), 0) + 1
    inv_pos = 1.0 / pos1.astype(_F32)
    for j in range(B_WIDTH // SLAB):
        lanes = slice(j * SLAB, (j + 1) * SLAB)
        xp, zp = [proj(4 * A_WIDTH + k * B_WIDTH + j * SLAB) for k in range(2)]
        xp = xp * r
        carry = jnp.where(first, 0.0, xp_carry[:, lanes])
        xp_carry[:, lanes] = xp[tm - POOL_HALO:, :]
        acc = jnp.concatenate([carry, xp], axis=0)
        shift = 1
        for _ in range(j * groups_per_slab):
            acc = acc + pltpu.roll(acc, shift, axis=0)
            shift *= 2
        pooled = []
        for k in range(groups_per_slab):
            w = POOL_WINDOWS[j * groups_per_slab + k]
            acc = acc + pltpu.roll(acc, shift, axis=0)
            shift *= 2
            inv_count = jnp.where(pos1 < w, inv_pos, 1.0 / w)
            pooled.append(acc[POOL_HALO:, :GROUP_DIM] * inv_count
                          - xp[:, k * GROUP_DIM:(k + 1) * GROUP_DIM])
            acc = acc[:, GROUP_DIM:]
        pooled = jnp.concatenate(pooled, axis=-1).astype(_BF16)
        mixed = _dot(pooled, poolw_ref[j])
        mix_ref[:, A_WIDTH + j * SLAB:A_WIDTH + (j + 1) * SLAB] = (
            mixed * pscale_ref[:, lanes] * _silu(zp * r)).astype(_BF16)


def _odd_mix(xg, r, mix_ref, t, win_ref, lng_ref, lnb_ref, ws_ref, bias_ref):
    del t
    tm = xg.shape[0]
    n_chunks = tm // CHUNK
    n_slabs = D_MODEL // SLAB
    heads_per_slab = SLAB // C_HEAD_DIM

    def proj(col):
        return _dot(xg, win_ref[:, col:col + SLAB])

    pv = [proj(D_MODEL + j * SLAB) for j in range(n_slabs)]
    mu = jnp.sum((pv[0] + pv[1]) + (pv[2] + pv[3]), axis=-1,
                 keepdims=True) * (1.0 / D_MODEL)
    vc = [p - mu for p in pv]
    sq = [c * c for c in vc]
    var = jnp.sum((sq[0] + sq[1]) + (sq[2] + sq[3]), axis=-1,
                  keepdims=True) * (1.0 / D_MODEL)
    rstd = r * lax.rsqrt(r * r * var + EPS)

    row = lax.broadcasted_iota(jnp.int32, (CHUNK, CHUNK), 0)
    col = lax.broadcasted_iota(jnp.int32, (CHUNK, CHUNK), 1)
    for j in range(n_slabs):
        lanes = slice(j * SLAB, (j + 1) * SLAB)
        vn = (vc[j] * rstd * lng_ref[:, lanes] + lnb_ref[:, lanes]).astype(_BF16)
        sv = []
        for k in range(heads_per_slab):
            hd = j * heads_per_slab + k
            ws = jnp.where(col <= row, ws_ref[hd], 0.0).astype(_BF16)
            hl = slice(k * C_HEAD_DIM, (k + 1) * C_HEAD_DIM)
            rhs = jnp.concatenate(
                [vn[n * CHUNK:(n + 1) * CHUNK, hl] for n in range(n_chunks)],
                axis=-1)
            out = _dot(ws, rhs)
            sv.append(jnp.concatenate(
                [out[:, n * C_HEAD_DIM:(n + 1) * C_HEAD_DIM]
                 for n in range(n_chunks)], axis=0))
        bias = bias_ref[:, lanes]
        sv = (jnp.concatenate(sv, axis=-1)
              + jnp.concatenate([bias] * n_chunks, axis=0))
        u = proj(j * SLAB)
        z = proj(2 * D_MODEL + j * SLAB)
        mix_ref[:, lanes] = (u * r * sv * _silu(z * r)).astype(_BF16)


def _layer_kernel(mix_fn, n_params, *refs):
    x_ref, pre_ref, post_ref, win_ref = refs[:4]
    params = refs[4:4 + n_params]
    (wout_ref, o_ref, mix_scr, win_scr,
     wout_scr) = refs[4 + n_params:9 + n_params]
    carries = refs[9 + n_params:]
    tm = x_ref.shape[1]
    t = pl.program_id(1)

    @pl.when(jnp.logical_and(pl.program_id(0) == 0, t == 0))
    def _():
        win_scr[...] = win_ref[...]
        wout_scr[...] = wout_ref[...]

    x = x_ref[0]
    xg = (x * pre_ref[...]).astype(_BF16)
    mix_fn(xg, _row_scale(x), mix_scr, t, win_scr, *params, *carries)

    for c in range(tm // POST_ROWS):
        rows = slice(c * POST_ROWS, (c + 1) * POST_ROWS)
        m = _dot(mix_scr[rows, :], wout_scr[...])
        o_ref[0, rows, :] = (x_ref[0, rows, :]
                             + m * _row_scale(m) * post_ref[...])


def _full(shape):
    return pl.BlockSpec(shape, lambda b, t: (0,) * len(shape),
                        pipeline_mode=pl.Buffered(1))


def _layer_call(name, mix_fn, x, pre, post, w_in, params, w_out, carry_shapes):
    bsz, seq, d = x.shape
    row_spec = pl.BlockSpec((1, ROW_TILE, d), lambda b, t: (b, t, 0))
    operands = (x, pre.reshape(1, -1), post.reshape(1, -1), w_in, *params,
                w_out)
    scratch = [pltpu.VMEM((ROW_TILE, d), _BF16),
               pltpu.VMEM(w_in.shape, _BF16),
               pltpu.VMEM(w_out.shape, _BF16)]
    scratch += [pltpu.VMEM(s, _F32) for s in carry_shapes]
    return pl.pallas_call(
        functools.partial(_layer_kernel, mix_fn, len(params)),
        name=name,
        grid=(bsz, seq // ROW_TILE),
        in_specs=[row_spec] + [_full(p.shape) for p in operands[1:]],
        out_specs=row_spec,
        out_shape=jax.ShapeDtypeStruct(x.shape, x.dtype),
        scratch_shapes=scratch,
        compiler_params=pltpu.CompilerParams(
            dimension_semantics=("arbitrary", "arbitrary"),
            vmem_limit_bytes=VMEM_LIMIT_BYTES),
    )(*operands)


def _even_layer(x, pre, post, w_in, conv_w, pool_w, pool_scale, w_out):
    pw = pool_w.astype(_BF16)
    zero = jnp.zeros_like(pw[0])
    pool_bd = jnp.stack([
        jnp.block([[pw[2 * j], zero], [zero, pw[2 * j + 1]]])
        for j in range(B_WIDTH // SLAB)])
    params = (conv_w, pool_bd, pool_scale.reshape(1, -1))
    carries = ((CONV_HALO, A_WIDTH), (POOL_HALO, B_WIDTH))
    return _layer_call("even_layer", _even_mix, x, pre, post,
                       w_in.astype(_BF16), params, w_out.astype(_BF16), carries)


def _odd_layer(x, pre, post, w_in, ln_g, ln_b, w_s, b_s, w_out):
    bias = jnp.repeat(b_s.T, C_HEAD_DIM, axis=1)
    params = (ln_g.reshape(1, -1), ln_b.reshape(1, -1), w_s, bias)
    return _layer_call("odd_layer", _odd_mix, x, pre, post,
                       w_in.astype(_BF16), params, w_out.astype(_BF16), ())


def kernel(x, pre_norm, post_norm, even_w_in, even_conv_w, even_pool_w, even_pool_scale, even_w_out, odd_w_in, odd_ln_g, odd_ln_b, odd_w_s, odd_b_s, odd_w_out):
    depth = pre_norm.shape[0]
    for i in range(depth):
        j = i // 2
        if i % 2 == 0:
            x = _even_layer(x, pre_norm[i], post_norm[i], even_w_in[j],
                            even_conv_w[j], even_pool_w[j], even_pool_scale[j],
                            even_w_out[j])
        else:
            x = _odd_layer(x, pre_norm[i], post_norm[i], odd_w_in[j],
                           odd_ln_g[j], odd_ln_b[j], odd_w_s[j], odd_b_s[j],
                           odd_w_out[j])
    return x
```

```python
import functools

import jax
import jax.numpy as jnp
from jax import lax
from jax.experimental import pallas as pl
from jax.experimental.pallas import tpu as pltpu

D_MODEL = 1024
A_WIDTH = 512
B_WIDTH = 512
POOL_WINDOWS = (2, 4, 8, 16)
GROUP_DIM = B_WIDTH // len(POOL_WINDOWS)
C_HEADS = 8
C_HEAD_DIM = 128
CHUNK = 128
EPS = 1e-6

ROW_TILE = 1024
POST_ROWS = 512
SLAB = 256
CONV_HALO = 8
POOL_HALO = 16
VMEM_LIMIT_BYTES = 56 * 1024 * 1024

_F32 = jnp.float32
_BF16 = jnp.bfloat16


def _silu(v):
    half = 0.5 * v
    return half + half * jnp.tanh(half)


def _dot(a, b):
    return jnp.dot(a, b, preferred_element_type=_F32)


def _row_scale(v):
    return lax.rsqrt(jnp.mean(v * v, axis=-1, keepdims=True) + EPS)


def _even_mix(xg, r, mix_ref, t, win_ref, convw_ref, poolw_ref, pscale_ref,
              hc_carry, xp_carry):
    tm = xg.shape[0]
    first = t == 0
    groups_per_slab = SLAB // GROUP_DIM
    r2 = r * r

    def proj(col):
        return _dot(xg, win_ref[:, col:col + SLAB])

    for j in range(A_WIDTH // SLAB):
        lanes = slice(j * SLAB, (j + 1) * SLAB)
        xa, gb, gc, za = [proj(k * A_WIDTH + j * SLAB) for k in range(4)]
        cw = convw_ref[:, lanes]
        hc = gc * xa * r2
        carry = jnp.where(first, 0.0, hc_carry[:, lanes])
        hc_carry[:, lanes] = hc[tm - CONV_HALO:, :]
        ext = jnp.concatenate([carry, hc], axis=0)
        hc1 = pltpu.roll(ext, 1, axis=0)[CONV_HALO:, :]
        hc2 = pltpu.roll(ext, 2, axis=0)[CONV_HALO:, :]
        conv = cw[0:1] * hc2 + cw[1:2] * hc1 + cw[2:3] * hc
        mix_ref[:, lanes] = (gb * r * conv * _silu(za * r)).astype(_BF16)

    pos1 = t * tm + lax.broadcasted_iota(jnp.int32, (tm, GROUP_DIM), 0) + 1
    inv_pos = 1.0 / pos1.astype(_F32)
    for j in range(B_WIDTH // SLAB):
        lanes = slice(j * SLAB, (j + 1) * SLAB)
        xp, zp = [proj(4 * A_WIDTH + k * B_WIDTH + j * SLAB) for k in range(2)]
        xp = xp * r
        carry = jnp.where(first, 0.0, xp_carry[:, lanes])
        xp_carry[:, lanes] = xp[tm - POOL_HALO:, :]
        acc = jnp.concatenate([carry, xp], axis=0)
        shift = 1
        for _ in range(j * groups_per_slab):
            acc = acc + pltpu.roll(acc, shift, axis=0)
            shift *= 2
        pooled = []
        for k in range(groups_per_slab):
            w = POOL_WINDOWS[j * groups_per_slab + k]
            acc = acc + pltpu.roll(acc, shift, axis=0)
            shift *= 2
            inv_count = jnp.where(pos1 < w, inv_pos, 1.0 / w)
            pooled.append(acc[POOL_HALO:, :GROUP_DIM] * inv_count
                          - xp[:, k * GROUP_DIM:(k + 1) * GROUP_DIM])
            acc = acc[:, GROUP_DIM:]
        pooled = jnp.concatenate(pooled, axis=-1).astype(_BF16)
        mixed = _dot(pooled, poolw_ref[j])
        mix_ref[:, A_WIDTH + j * SLAB:A_WIDTH + (j + 1) * SLAB] = (
            mixed * pscale_ref[:, lanes] * _silu(zp * r)).astype(_BF16)


def _odd_mix(xg, r, mix_ref, t, win_ref, lng_ref, lnb_ref, ws_ref, bias_ref):
    del t
    tm = xg.shape[0]
    n_chunks = tm // CHUNK
    n_slabs = D_MODEL // SLAB
    heads_per_slab = SLAB // C_HEAD_DIM

    def proj(col):
        return _dot(xg, win_ref[:, col:col + SLAB])

    pv = [proj(D_MODEL + j * SLAB) for j in range(n_slabs)]
    mu = jnp.sum((pv[0] + pv[1]) + (pv[2] + pv[3]), axis=-1,
                 keepdims=True) * (1.0 / D_MODEL)
    vc = [p - mu for p in pv]
    sq = [c * c for c in vc]
    var = jnp.sum((sq[0] + sq[1]) + (sq[2] + sq[3]), axis=-1,
                  keepdims=True) * (1.0 / D_MODEL)
    rstd = r * lax.rsqrt(r * r * var + EPS)

    row = lax.broadcasted_iota(jnp.int32, (CHUNK, CHUNK), 0)
    col = lax.broadcasted_iota(jnp.int32, (CHUNK, CHUNK), 1)
    for j in range(n_slabs):
        lanes = slice(j * SLAB, (j + 1) * SLAB)
        vn = (vc[j] * rstd * lng_ref[:, lanes] + lnb_ref[:, lanes]).astype(_BF16)
        sv = []
        for k in range(heads_per_slab):
            hd = j * heads_per_slab + k
            ws = jnp.where(col <= row, ws_ref[hd], 0.0).astype(_BF16)
            hl = slice(k * C_HEAD_DIM, (k + 1) * C_HEAD_DIM)
            rhs = jnp.concatenate(
                [vn[n * CHUNK:(n + 1) * CHUNK, hl] for n in range(n_chunks)],
                axis=-1)
            out = _dot(ws, rhs)
            sv.append(jnp.concatenate(
                [out[:, n * C_HEAD_DIM:(n + 1) * C_HEAD_DIM]
                 for n in range(n_chunks)], axis=0))
        bias = bias_ref[:, lanes]
        sv = (jnp.concatenate(sv, axis=-1)
              + jnp.concatenate([bias] * n_chunks, axis=0))
        u = proj(j * SLAB)
        z = proj(2 * D_MODEL + j * SLAB)
        mix_ref[:, lanes] = (u * r * sv * _silu(z * r)).astype(_BF16)


def _layer_kernel(mix_fn, n_params, *refs):
    x_ref, pre_ref, post_ref, win_ref = refs[:4]
    params = refs[4:4 + n_params]
    (wout_ref, o_ref, mix_scr, win_scr,
     wout_scr) = refs[4 + n_params:9 + n_params]
    carries = refs[9 + n_params:]
    tm = x_ref.shape[1]
    t = pl.program_id(1)

    @pl.when(jnp.logical_and(pl.program_id(0) == 0, t == 0))
    def _():
        win_scr[...] = win_ref[...].astype(_BF16)
        wout_scr[...] = wout_ref[...].astype(_BF16)

    x = x_ref[0]
    xg = (x * pre_ref[...]).astype(_BF16)
    mix_fn(xg, _row_scale(x), mix_scr, t, win_scr, *params, *carries)

    for c in range(tm // POST_ROWS):
        rows = slice(c * POST_ROWS, (c + 1) * POST_ROWS)
        m = _dot(mix_scr[rows, :], wout_scr[...])
        o_ref[0, rows, :] = (x_ref[0, rows, :]
                             + m * _row_scale(m) * post_ref[...])


def _full(shape):
    return pl.BlockSpec(shape, lambda b, t: (0,) * len(shape),
                        pipeline_mode=pl.Buffered(1))


def _layer_call(name, mix_fn, x, pre, post, w_in, params, w_out, carry_shapes):
    bsz, seq, d = x.shape
    row_spec = pl.BlockSpec((1, ROW_TILE, d), lambda b, t: (b, t, 0))
    operands = (x, pre.reshape(1, -1), post.reshape(1, -1), w_in, *params,
                w_out)
    scratch = [pltpu.VMEM((ROW_TILE, d), _BF16),
               pltpu.VMEM(w_in.shape, _BF16),
               pltpu.VMEM(w_out.shape, _BF16)]
    scratch += [pltpu.VMEM(s, _F32) for s in carry_shapes]
    return pl.pallas_call(
        functools.partial(_layer_kernel, mix_fn, len(params)),
        name=name,
        grid=(bsz, seq // ROW_TILE),
        in_specs=[row_spec] + [_full(p.shape) for p in operands[1:]],
        out_specs=row_spec,
        out_shape=jax.ShapeDtypeStruct(x.shape, x.dtype),
        scratch_shapes=scratch,
        compiler_params=pltpu.CompilerParams(
            dimension_semantics=("arbitrary", "arbitrary"),
            vmem_limit_bytes=VMEM_LIMIT_BYTES),
    )(*operands)


def _even_layer(x, pre, post, w_in, conv_w, pool_w, pool_scale, w_out):
    pw = pool_w.astype(_BF16)
    zero = jnp.zeros_like(pw[0])
    pool_bd = jnp.stack([
        jnp.block([[pw[2 * j], zero], [zero, pw[2 * j + 1]]])
        for j in range(B_WIDTH // SLAB)])
    params = (conv_w, pool_bd, pool_scale.reshape(1, -1))
    carries = ((CONV_HALO, A_WIDTH), (POOL_HALO, B_WIDTH))
    return _layer_call("even_layer", _even_mix, x, pre, post, w_in, params,
                       w_out, carries)


def _odd_layer(x, pre, post, w_in, ln_g, ln_b, w_s, b_s, w_out):
    bias = jnp.repeat(b_s.T, C_HEAD_DIM, axis=1)
    params = (ln_g.reshape(1, -1), ln_b.reshape(1, -1), w_s, bias)
    return _layer_call("odd_layer", _odd_mix, x, pre, post, w_in, params,
                       w_out, ())


def kernel(x, pre_norm, post_norm, even_w_in, even_conv_w, even_pool_w, even_pool_scale, even_w_out, odd_w_in, odd_ln_g, odd_ln_b, odd_w_s, odd_b_s, odd_w_out):
    depth = pre_norm.shape[0]
    for i in range(depth):
        j = i // 2
        if i % 2 == 0:
            x = _even_layer(x, pre_norm[i], post_norm[i], even_w_in[j],
                            even_conv_w[j], even_pool_w[j], even_pool_scale[j],
                            even_w_out[j])
        else:
            x = _odd_layer(x, pre_norm[i], post_norm[i], odd_w_in[j],
                           odd_ln_g[j], odd_ln_b[j], odd_w_s[j], odd_b_s[j],
                           odd_w_out[j])
    return x
```

```python
import functools

import jax
import jax.numpy as jnp
from jax import lax
from jax.experimental import pallas as pl
from jax.experimental.pallas import tpu as pltpu

D_MODEL = 1024
A_WIDTH = 512
B_WIDTH = 512
POOL_WINDOWS = (2, 4, 8, 16)
GROUP_DIM = B_WIDTH // len(POOL_WINDOWS)
C_HEADS = 8
C_HEAD_DIM = 128
CHUNK = 128
EPS = 1e-6

ROW_TILE = 1024
POST_ROWS = (256, 256, 256, 256)
SLAB = 256
CONV_HALO = 8
POOL_HALO = 16
VMEM_LIMIT_BYTES = 56 * 1024 * 1024

_A, _B = A_WIDTH, B_WIDTH
EVEN_ORDER = (("a", 0), ("a", 1), ("b", 0), ("b", 1))


def _even_cols(kind, j):
    if kind == "a":
        return (0 * _A + j * SLAB, 2 * _A + j * SLAB,
                1 * _A + j * SLAB, 3 * _A + j * SLAB)
    return (4 * _A + j * SLAB, 4 * _A + _B + j * SLAB)


EVEN_SLAB_COLS = tuple(c for kind, j in EVEN_ORDER for c in _even_cols(kind, j))
ODD_SLAB_COLS = tuple(
    D_MODEL + j * SLAB for j in range(D_MODEL // SLAB)
) + tuple(
    col for j in range(D_MODEL // SLAB)
    for col in (j * SLAB, 2 * D_MODEL + j * SLAB))
EVEN_MIX_ROWS = tuple((0 if kind == "a" else _A) + j * SLAB
                      for kind, j in EVEN_ORDER)
ODD_MIX_ROWS = tuple(j * SLAB for j in range(D_MODEL // SLAB))

_F32 = jnp.float32
_BF16 = jnp.bfloat16


def _silu(v):
    half = 0.5 * v
    return half + half * jnp.tanh(half)


def _dot(a, b):
    return jnp.dot(a, b, preferred_element_type=_F32)


def _row_scale(v):
    return lax.rsqrt(jnp.mean(v * v, axis=-1, keepdims=True) + EPS)


def _proj_pair(xg, win_ref, pair):
    out = _dot(xg, win_ref[:, 2 * pair * SLAB:(2 * pair + 2) * SLAB])
    return out[:, :SLAB], out[:, SLAB:]


def _even_mix(xg, r, mix_ref, t, win_ref, convw_ref, poolw_ref, pscale_ref,
              hc_carry, xp_carry):
    tm = xg.shape[0]
    first = t == 0
    groups_per_slab = SLAB // GROUP_DIM
    r2 = r * r
    pos1 = t * tm + lax.broadcasted_iota(jnp.int32, (tm, GROUP_DIM), 0) + 1
    inv_pos = 1.0 / pos1.astype(_F32)

    def slab_b(j, pair, out_lanes):
        lanes = slice(j * SLAB, (j + 1) * SLAB)
        xp, zp = _proj_pair(xg, win_ref, pair)
        xp = xp * r
        carry = jnp.where(first, 0.0, xp_carry[:, lanes])
        xp_carry[:, lanes] = xp[tm - POOL_HALO:, :]
        acc = jnp.concatenate([carry, xp], axis=0)
        shift = 1
        for _ in range(j * groups_per_slab):
            acc = acc + pltpu.roll(acc, shift, axis=0)
            shift *= 2
        pooled = []
        for k in range(groups_per_slab):
            w = POOL_WINDOWS[j * groups_per_slab + k]
            acc = acc + pltpu.roll(acc, shift, axis=0)
            shift *= 2
            inv_count = jnp.where(pos1 < w, inv_pos, 1.0 / w)
            pooled.append(acc[POOL_HALO:, :GROUP_DIM] * inv_count
                          - xp[:, k * GROUP_DIM:(k + 1) * GROUP_DIM])
            acc = acc[:, GROUP_DIM:]
        pooled = jnp.concatenate(pooled, axis=-1).astype(_BF16)
        mixed = _dot(pooled, poolw_ref[j])
        mix_ref[:, out_lanes] = (
            mixed * pscale_ref[:, lanes] * _silu(zp * r)).astype(_BF16)

    def slab_a(j, pair, out_lanes):
        lanes = slice(j * SLAB, (j + 1) * SLAB)
        xa, gc = _proj_pair(xg, win_ref, pair)
        cw = convw_ref[:, lanes]
        hc = gc * xa * r2
        carry = jnp.where(first, 0.0, hc_carry[:, lanes])
        hc_carry[:, lanes] = hc[tm - CONV_HALO:, :]
        ext = jnp.concatenate([carry, hc], axis=0)
        hc1 = pltpu.roll(ext, 1, axis=0)[CONV_HALO:, :]
        hc2 = pltpu.roll(ext, 2, axis=0)[CONV_HALO:, :]
        conv = cw[0:1] * hc2 + cw[1:2] * hc1 + cw[2:3] * hc
        gb, za = _proj_pair(xg, win_ref, pair + 1)
        mix_ref[:, out_lanes] = (gb * r * conv * _silu(za * r)).astype(_BF16)

    pair = 0
    for pos, (kind, j) in enumerate(EVEN_ORDER):
        out_lanes = slice(pos * SLAB, (pos + 1) * SLAB)
        if kind == "a":
            slab_a(j, pair, out_lanes)
            pair += 2
        else:
            slab_b(j, pair, out_lanes)
            pair += 1


def _odd_mix(xg, r, mix_ref, t, win_ref, lng_ref, lnb_ref, ws_ref, bias_ref):
    del t
    tm = xg.shape[0]
    n_chunks = tm // CHUNK
    n_slabs = D_MODEL // SLAB
    heads_per_slab = SLAB // C_HEAD_DIM

    pv = [p for pair in range(n_slabs // 2)
          for p in _proj_pair(xg, win_ref, pair)]
    mu = jnp.sum((pv[0] + pv[1]) + (pv[2] + pv[3]), axis=-1,
                 keepdims=True) * (1.0 / D_MODEL)
    vc = [p - mu for p in pv]
    sq = [c * c for c in vc]
    var = jnp.sum((sq[0] + sq[1]) + (sq[2] + sq[3]), axis=-1,
                  keepdims=True) * (1.0 / D_MODEL)
    rstd = r * lax.rsqrt(r * r * var + EPS)

    row = lax.broadcasted_iota(jnp.int32, (CHUNK, CHUNK), 0)
    col = lax.broadcasted_iota(jnp.int32, (CHUNK, CHUNK), 1)
    for j in range(n_slabs):
        lanes = slice(j * SLAB, (j + 1) * SLAB)
        vn = (vc[j] * rstd * lng_ref[:, lanes] + lnb_ref[:, lanes]).astype(_BF16)
        sv = []
        for k in range(heads_per_slab):
            hd = j * heads_per_slab + k
            ws = jnp.where(col <= row, ws_ref[hd], 0.0).astype(_BF16)
            hl = slice(k * C_HEAD_DIM, (k + 1) * C_HEAD_DIM)
            rhs = jnp.concatenate(
                [vn[n * CHUNK:(n + 1) * CHUNK, hl] for n in range(n_chunks)],
                axis=-1)
            out = _dot(ws, rhs)
            sv.append(jnp.concatenate(
                [out[:, n * C_HEAD_DIM:(n + 1) * C_HEAD_DIM]
                 for n in range(n_chunks)], axis=0))
        bias = bias_ref[:, lanes]
        sv = (jnp.concatenate(sv, axis=-1)
              + jnp.concatenate([bias] * n_chunks, axis=0))
        u, z = _proj_pair(xg, win_ref, n_slabs // 2 + j)
        mix_ref[:, lanes] = (u * r * sv * _silu(z * r)).astype(_BF16)


def _layer_kernel(mix_fn, slab_cols, mix_rows, n_params, *refs):
    x_ref, pre_ref, post_ref, win_ref = refs[:4]
    params = refs[4:4 + n_params]
    (wout_ref, o_ref, mix_scr, win_scr,
     wout_scr) = refs[4 + n_params:9 + n_params]
    carries = refs[9 + n_params:]
    tm = x_ref.shape[1]
    t = pl.program_id(1)

    @pl.when(jnp.logical_and(pl.program_id(0) == 0, t == 0))
    def _():
        for pos, col in enumerate(slab_cols):
            win_scr[:, pos * SLAB:(pos + 1) * SLAB] = (
                win_ref[:, col:col + SLAB].astype(_BF16))
        for pos, row in enumerate(mix_rows):
            wout_scr[pos * SLAB:(pos + 1) * SLAB, :] = (
                wout_ref[row:row + SLAB, :].astype(_BF16))

    x = x_ref[0]
    xg = (x * pre_ref[...]).astype(_BF16)
    mix_fn(xg, _row_scale(x), mix_scr, t, win_scr, *params, *carries)

    assert sum(POST_ROWS) == tm
    start = 0
    for n_rows in POST_ROWS:
        rows = slice(start, start + n_rows)
        start += n_rows
        m = _dot(mix_scr[rows, :], wout_scr[...])
        o_ref[0, rows, :] = (x_ref[0, rows, :]
                             + m * _row_scale(m) * post_ref[...])


def _full(shape):
    return pl.BlockSpec(shape, lambda b, t: (0,) * len(shape),
                        pipeline_mode=pl.Buffered(1))


def _layer_call(name, mix_fn, slab_cols, mix_rows, x, pre, post, w_in, params,
                w_out, carry_shapes):
    bsz, seq, d = x.shape
    assert sorted(slab_cols) == list(range(0, w_in.shape[1], SLAB))
    assert sorted(mix_rows) == list(range(0, w_out.shape[0], SLAB))
    row_spec = pl.BlockSpec((1, ROW_TILE, d), lambda b, t: (b, t, 0))
    operands = (x, pre.reshape(1, -1), post.reshape(1, -1), w_in, *params,
                w_out)
    scratch = [pltpu.VMEM((ROW_TILE, d), _BF16),
               pltpu.VMEM(w_in.shape, _BF16),
               pltpu.VMEM(w_out.shape, _BF16)]
    scratch += [pltpu.VMEM(s, _F32) for s in carry_shapes]
    return pl.pallas_call(
        functools.partial(_layer_kernel, mix_fn, slab_cols, mix_rows,
                          len(params)),
        name=name,
        grid=(bsz, seq // ROW_TILE),
        in_specs=[row_spec] + [_full(p.shape) for p in operands[1:]],
        out_specs=row_spec,
        out_shape=jax.ShapeDtypeStruct(x.shape, x.dtype),
        scratch_shapes=scratch,
        compiler_params=pltpu.CompilerParams(
            dimension_semantics=("arbitrary", "arbitrary"),
            vmem_limit_bytes=VMEM_LIMIT_BYTES),
    )(*operands)


def _even_layer(x, pre, post, w_in, conv_w, pool_w, pool_scale, w_out):
    pw = pool_w.astype(_BF16)
    zero = jnp.zeros_like(pw[0])
    pool_bd = jnp.stack([
        jnp.block([[pw[2 * j], zero], [zero, pw[2 * j + 1]]])
        for j in range(B_WIDTH // SLAB)])
    params = (conv_w, pool_bd, pool_scale.reshape(1, -1))
    carries = ((CONV_HALO, A_WIDTH), (POOL_HALO, B_WIDTH))
    return _layer_call("even_layer", _even_mix, EVEN_SLAB_COLS, EVEN_MIX_ROWS,
                       x, pre, post, w_in, params, w_out, carries)


def _odd_layer(x, pre, post, w_in, ln_g, ln_b, w_s, b_s, w_out):
    bias = jnp.repeat(b_s.T, C_HEAD_DIM, axis=1)
    params = (ln_g.reshape(1, -1), ln_b.reshape(1, -1), w_s, bias)
    return _layer_call("odd_layer", _odd_mix, ODD_SLAB_COLS, ODD_MIX_ROWS,
                       x, pre, post, w_in, params, w_out, ())


def kernel(x, pre_norm, post_norm, even_w_in, even_conv_w, even_pool_w, even_pool_scale, even_w_out, odd_w_in, odd_ln_g, odd_ln_b, odd_w_s, odd_b_s, odd_w_out):
    depth = pre_norm.shape[0]
    for i in range(depth):
        j = i // 2
        if i % 2 == 0:
            x = _even_layer(x, pre_norm[i], post_norm[i], even_w_in[j],
                            even_conv_w[j], even_pool_w[j], even_pool_scale[j],
                            even_w_out[j])
        else:
            x = _odd_layer(x, pre_norm[i], post_norm[i], odd_w_in[j],
                           odd_ln_g[j], odd_ln_b[j], odd_w_s[j], odd_b_s[j],
                           odd_w_out[j])
    return x
```

```python
import functools

import jax
import jax.numpy as jnp
from jax import lax
from jax.experimental import pallas as pl
from jax.experimental.pallas import tpu as pltpu

D_MODEL = 1024
A_WIDTH = 512
B_WIDTH = 512
POOL_WINDOWS = (2, 4, 8, 16)
GROUP_DIM = B_WIDTH // len(POOL_WINDOWS)
C_HEADS = 8
C_HEAD_DIM = 128
CHUNK = 128
EPS = 1e-6

ROW_TILE = 1024
POST_ROWS = (1024,)
SLAB = 256
CONV_HALO = 8
POOL_HALO = 16
VMEM_LIMIT_BYTES = 56 * 1024 * 1024

_A, _B = A_WIDTH, B_WIDTH
EVEN_ORDER = (("a", 0), ("a", 1), ("b", 0), ("b", 1))


def _even_cols(kind, j):
    if kind == "a":
        return (0 * _A + j * SLAB, 2 * _A + j * SLAB,
                1 * _A + j * SLAB, 3 * _A + j * SLAB)
    return (4 * _A + j * SLAB, 4 * _A + _B + j * SLAB)


EVEN_SLAB_COLS = tuple(c for kind, j in EVEN_ORDER for c in _even_cols(kind, j))
ODD_SLAB_COLS = tuple(
    D_MODEL + j * SLAB for j in range(D_MODEL // SLAB)
) + tuple(
    col for j in range(D_MODEL // SLAB)
    for col in (j * SLAB, 2 * D_MODEL + j * SLAB))
EVEN_MIX_ROWS = tuple((0 if kind == "a" else _A) + j * SLAB
                      for kind, j in EVEN_ORDER)
ODD_MIX_ROWS = tuple(j * SLAB for j in range(D_MODEL // SLAB))

_F32 = jnp.float32
_BF16 = jnp.bfloat16


def _silu(v):
    half = 0.5 * v
    return half + half * jnp.tanh(half)


def _dot(a, b):
    return jnp.dot(a, b, preferred_element_type=_F32)


def _row_scale(v):
    return lax.rsqrt(jnp.mean(v * v, axis=-1, keepdims=True) + EPS)


def _proj_pair(xg, win_ref, pair):
    out = _dot(xg, win_ref[:, 2 * pair * SLAB:(2 * pair + 2) * SLAB])
    return out[:, :SLAB], out[:, SLAB:]


def _even_mix(xg, r, mix_ref, t, win_ref, convw_ref, poolw_ref, pscale_ref,
              hc_carry, xp_carry):
    tm = xg.shape[0]
    first = t == 0
    groups_per_slab = SLAB // GROUP_DIM
    r2 = r * r
    pos1 = t * tm + lax.broadcasted_iota(jnp.int32, (tm, GROUP_DIM), 0) + 1
    inv_pos = 1.0 / pos1.astype(_F32)

    def slab_b(j, pair, out_lanes):
        lanes = slice(j * SLAB, (j + 1) * SLAB)
        xp, zp = _proj_pair(xg, win_ref, pair)
        xp = xp * r
        carry = jnp.where(first, 0.0, xp_carry[:, lanes])
        xp_carry[:, lanes] = xp[tm - POOL_HALO:, :]
        acc = jnp.concatenate([carry, xp], axis=0)
        shift = 1
        for _ in range(j * groups_per_slab):
            acc = acc + pltpu.roll(acc, shift, axis=0)
            shift *= 2
        pooled = []
        for k in range(groups_per_slab):
            w = POOL_WINDOWS[j * groups_per_slab + k]
            acc = acc + pltpu.roll(acc, shift, axis=0)
            shift *= 2
            inv_count = jnp.where(pos1 < w, inv_pos, 1.0 / w)
            pooled.append(acc[POOL_HALO:, :GROUP_DIM] * inv_count
                          - xp[:, k * GROUP_DIM:(k + 1) * GROUP_DIM])
            acc = acc[:, GROUP_DIM:]
        pooled = jnp.concatenate(pooled, axis=-1).astype(_BF16)
        mixed = _dot(pooled, poolw_ref[j])
        mix_ref[:, out_lanes] = (
            mixed * pscale_ref[:, lanes] * _silu(zp * r)).astype(_BF16)

    def slab_a(j, pair, out_lanes):
        lanes = slice(j * SLAB, (j + 1) * SLAB)
        xa, gc = _proj_pair(xg, win_ref, pair)
        cw = convw_ref[:, lanes]
        hc = gc * xa * r2
        carry = jnp.where(first, 0.0, hc_carry[:, lanes])
        hc_carry[:, lanes] = hc[tm - CONV_HALO:, :]
        ext = jnp.concatenate([carry, hc], axis=0)
        hc1 = pltpu.roll(ext, 1, axis=0)[CONV_HALO:, :]
        hc2 = pltpu.roll(ext, 2, axis=0)[CONV_HALO:, :]
        conv = cw[0:1] * hc2 + cw[1:2] * hc1 + cw[2:3] * hc
        gb, za = _proj_pair(xg, win_ref, pair + 1)
        mix_ref[:, out_lanes] = (gb * r * conv * _silu(za * r)).astype(_BF16)

    pair = 0
    for pos, (kind, j) in enumerate(EVEN_ORDER):
        out_lanes = slice(pos * SLAB, (pos + 1) * SLAB)
        if kind == "a":
            slab_a(j, pair, out_lanes)
            pair += 2
        else:
            slab_b(j, pair, out_lanes)
            pair += 1


def _odd_mix(xg, r, mix_ref, t, win_ref, lng_ref, lnb_ref, ws_ref, bias_ref):
    del t
    tm = xg.shape[0]
    n_chunks = tm // CHUNK
    n_slabs = D_MODEL // SLAB
    heads_per_slab = SLAB // C_HEAD_DIM

    pv = [p for pair in range(n_slabs // 2)
          for p in _proj_pair(xg, win_ref, pair)]
    mu = jnp.sum((pv[0] + pv[1]) + (pv[2] + pv[3]), axis=-1,
                 keepdims=True) * (1.0 / D_MODEL)
    vc = [p - mu for p in pv]
    sq = [c * c for c in vc]
    var = jnp.sum((sq[0] + sq[1]) + (sq[2] + sq[3]), axis=-1,
                  keepdims=True) * (1.0 / D_MODEL)
    rstd = r * lax.rsqrt(r * r * var + EPS)

    row = lax.broadcasted_iota(jnp.int32, (CHUNK, CHUNK), 0)
    col = lax.broadcasted_iota(jnp.int32, (CHUNK, CHUNK), 1)
    for j in range(n_slabs):
        lanes = slice(j * SLAB, (j + 1) * SLAB)
        vn = (vc[j] * rstd * lng_ref[:, lanes] + lnb_ref[:, lanes]).astype(_BF16)
        sv = []
        for k in range(heads_per_slab):
            hd = j * heads_per_slab + k
            ws = jnp.where(col <= row, ws_ref[hd], 0.0).astype(_BF16)
            hl = slice(k * C_HEAD_DIM, (k + 1) * C_HEAD_DIM)
            rhs = jnp.concatenate(
                [vn[n * CHUNK:(n + 1) * CHUNK, hl] for n in range(n_chunks)],
                axis=-1)
            out = _dot(ws, rhs)
            sv.append(jnp.concatenate(
                [out[:, n * C_HEAD_DIM:(n + 1) * C_HEAD_DIM]
                 for n in range(n_chunks)], axis=0))
        bias = bias_ref[:, lanes]
        sv = (jnp.concatenate(sv, axis=-1)
              + jnp.concatenate([bias] * n_chunks, axis=0))
        u, z = _proj_pair(xg, win_ref, n_slabs // 2 + j)
        mix_ref[:, lanes] = (u * r * sv * _silu(z * r)).astype(_BF16)


def _layer_kernel(mix_fn, slab_cols, mix_rows, n_params, *refs):
    x_ref, pre_ref, post_ref, win_ref = refs[:4]
    params = refs[4:4 + n_params]
    (wout_ref, o_ref, mix_scr, win_scr,
     wout_scr) = refs[4 + n_params:9 + n_params]
    carries = refs[9 + n_params:]
    tm = x_ref.shape[1]
    t = pl.program_id(1)

    @pl.when(jnp.logical_and(pl.program_id(0) == 0, t == 0))
    def _():
        for pos, col in enumerate(slab_cols):
            win_scr[:, pos * SLAB:(pos + 1) * SLAB] = (
                win_ref[:, col:col + SLAB].astype(_BF16))
        for pos, row in enumerate(mix_rows):
            wout_scr[pos * SLAB:(pos + 1) * SLAB, :] = (
                wout_ref[row:row + SLAB, :].astype(_BF16))

    x = x_ref[0]
    xg = (x * pre_ref[...]).astype(_BF16)
    mix_fn(xg, _row_scale(x), mix_scr, t, win_scr, *params, *carries)

    assert sum(POST_ROWS) == tm
    start = 0
    for n_rows in POST_ROWS:
        rows = slice(start, start + n_rows)
        start += n_rows
        m = _dot(mix_scr[rows, :], wout_scr[...])
        o_ref[0, rows, :] = (x_ref[0, rows, :]
                             + m * _row_scale(m) * post_ref[...])


def _full(shape):
    return pl.BlockSpec(shape, lambda b, t: (0,) * len(shape),
                        pipeline_mode=pl.Buffered(1))


def _layer_call(name, mix_fn, slab_cols, mix_rows, x, pre, post, w_in, params,
                w_out, carry_shapes):
    bsz, seq, d = x.shape
    assert sorted(slab_cols) == list(range(0, w_in.shape[1], SLAB))
    assert sorted(mix_rows) == list(range(0, w_out.shape[0], SLAB))
    row_spec = pl.BlockSpec((1, ROW_TILE, d), lambda b, t: (b, t, 0))
    operands = (x, pre.reshape(1, -1), post.reshape(1, -1), w_in, *params,
                w_out)
    scratch = [pltpu.VMEM((ROW_TILE, d), _BF16),
               pltpu.VMEM(w_in.shape, _BF16),
               pltpu.VMEM(w_out.shape, _BF16)]
    scratch += [pltpu.VMEM(s, _F32) for s in carry_shapes]
    return pl.pallas_call(
        functools.partial(_layer_kernel, mix_fn, slab_cols, mix_rows,
                          len(params)),
        name=name,
        grid=(bsz, seq // ROW_TILE),
        in_specs=[row_spec] + [_full(p.shape) for p in operands[1:]],
        out_specs=row_spec,
        out_shape=jax.ShapeDtypeStruct(x.shape, x.dtype),
        scratch_shapes=scratch,
        compiler_params=pltpu.CompilerParams(
            dimension_semantics=("arbitrary", "arbitrary"),
            vmem_limit_bytes=VMEM_LIMIT_BYTES),
    )(*operands)


def _even_layer(x, pre, post, w_in, conv_w, pool_w, pool_scale, w_out):
    pw = pool_w.astype(_BF16)
    zero = jnp.zeros_like(pw[0])
    pool_bd = jnp.stack([
        jnp.block([[pw[2 * j], zero], [zero, pw[2 * j + 1]]])
        for j in range(B_WIDTH // SLAB)])
    params = (conv_w, pool_bd, pool_scale.reshape(1, -1))
    carries = ((CONV_HALO, A_WIDTH), (POOL_HALO, B_WIDTH))
    return _layer_call("even_layer", _even_mix, EVEN_SLAB_COLS, EVEN_MIX_ROWS,
                       x, pre, post, w_in, params, w_out, carries)


def _odd_layer(x, pre, post, w_in, ln_g, ln_b, w_s, b_s, w_out):
    bias = jnp.repeat(b_s.T, C_HEAD_DIM, axis=1)
    params = (ln_g.reshape(1, -1), ln_b.reshape(1, -1), w_s, bias)
    return _layer_call("odd_layer", _odd_mix, ODD_SLAB_COLS, ODD_MIX_ROWS,
                       x, pre, post, w_in, params, w_out, ())


def kernel(x, pre_norm, post_norm, even_w_in, even_conv_w, even_pool_w, even_pool_scale, even_w_out, odd_w_in, odd_ln_g, odd_ln_b, odd_w_s, odd_b_s, odd_w_out):
    depth = pre_norm.shape[0]
    for i in range(depth):
        j = i // 2
        if i % 2 == 0:
            x = _even_layer(x, pre_norm[i], post_norm[i], even_w_in[j],
                            even_conv_w[j], even_pool_w[j], even_pool_scale[j],
                            even_w_out[j])
        else:
            x = _odd_layer(x, pre_norm[i], post_norm[i], odd_w_in[j],
                           odd_ln_g[j], odd_ln_b[j], odd_w_s[j], odd_b_s[j],
                           odd_w_out[j])
    return x
```

```python
import functools

import jax
import jax.numpy as jnp
from jax import lax
from jax.experimental import pallas as pl
from jax.experimental.pallas import tpu as pltpu

D_MODEL = 1024
A_WIDTH = 512
B_WIDTH = 512
POOL_WINDOWS = (2, 4, 8, 16)
GROUP_DIM = B_WIDTH // len(POOL_WINDOWS)
C_HEADS = 8
C_HEAD_DIM = 128
CHUNK = 128
EPS = 1e-6

ROW_TILE = 1024
POST_ROWS = (1024,)
SLAB = 256
CONV_HALO = 8
POOL_HALO = 16
VMEM_LIMIT_BYTES = 56 * 1024 * 1024

_A, _B = A_WIDTH, B_WIDTH
EVEN_ORDER = (("a", 0), ("a", 1), ("b", 0), ("b", 1))


def _even_cols(kind, j):
    if kind == "a":
        return (0 * _A + j * SLAB, 2 * _A + j * SLAB,
                1 * _A + j * SLAB, 3 * _A + j * SLAB)
    return (4 * _A + j * SLAB, 4 * _A + _B + j * SLAB)


EVEN_SLAB_COLS = tuple(c for kind, j in EVEN_ORDER for c in _even_cols(kind, j))
ODD_SLAB_COLS = tuple(
    D_MODEL + j * SLAB for j in range(D_MODEL // SLAB)
) + tuple(
    col for j in range(D_MODEL // SLAB)
    for col in (j * SLAB, 2 * D_MODEL + j * SLAB))
EVEN_MIX_ROWS = tuple((0 if kind == "a" else _A) + j * SLAB
                      for kind, j in EVEN_ORDER)
ODD_MIX_ROWS = tuple(j * SLAB for j in range(D_MODEL // SLAB))

_F32 = jnp.float32
_BF16 = jnp.bfloat16


def _silu_of_double(half):
    return half + half * jnp.tanh(half)


def _dot(a, b):
    return jnp.dot(a, b, preferred_element_type=_F32)


def _row_scale(v):
    return lax.rsqrt(jnp.mean(v * v, axis=-1, keepdims=True) + EPS)


def _proj_pair(xg, win_ref, pair):
    out = _dot(xg, win_ref[:, 2 * pair * SLAB:(2 * pair + 2) * SLAB])
    return out[:, :SLAB], out[:, SLAB:]


def _even_mix(xg, r, mix_ref, t, win_ref, convw_ref, poolw_ref, pscale_ref,
              hc_carry, xp_carry):
    tm = xg.shape[0]
    first = t == 0
    groups_per_slab = SLAB // GROUP_DIM
    r2 = r * r
    rh = 0.5 * r
    pos1 = t * tm + lax.broadcasted_iota(jnp.int32, (tm, GROUP_DIM), 0) + 1
    inv_pos = 1.0 / pos1.astype(_F32)

    def slab_b(j, pair, out_lanes):
        lanes = slice(j * SLAB, (j + 1) * SLAB)
        xp, zp = _proj_pair(xg, win_ref, pair)
        xp = xp * r
        carry = jnp.where(first, 0.0, xp_carry[:, lanes])
        xp_carry[:, lanes] = xp[tm - POOL_HALO:, :]
        acc = jnp.concatenate([carry, xp], axis=0)
        shift = 1
        for _ in range(j * groups_per_slab):
            acc = acc + pltpu.roll(acc, shift, axis=0)
            shift *= 2
        pooled = []
        for k in range(groups_per_slab):
            w = POOL_WINDOWS[j * groups_per_slab + k]
            acc = acc + pltpu.roll(acc, shift, axis=0)
            shift *= 2
            inv_count = jnp.where(pos1 < w, inv_pos, 1.0 / w)
            pooled.append(acc[POOL_HALO:, :GROUP_DIM] * inv_count
                          - xp[:, k * GROUP_DIM:(k + 1) * GROUP_DIM])
            acc = acc[:, GROUP_DIM:]
        pooled = jnp.concatenate(pooled, axis=-1).astype(_BF16)
        mixed = _dot(pooled, poolw_ref[j])
        mix_ref[:, out_lanes] = (
            mixed * pscale_ref[:, lanes] * _silu_of_double(zp * rh)
        ).astype(_BF16)

    def slab_a(j, pair, out_lanes):
        lanes = slice(j * SLAB, (j + 1) * SLAB)
        xa, gc = _proj_pair(xg, win_ref, pair)
        cw = convw_ref[:, lanes]
        hc = gc * xa * r2
        carry = jnp.where(first, 0.0, hc_carry[:, lanes])
        hc_carry[:, lanes] = hc[tm - CONV_HALO:, :]
        ext = jnp.concatenate([carry, hc], axis=0)
        hc1 = pltpu.roll(ext, 1, axis=0)[CONV_HALO:, :]
        hc2 = pltpu.roll(ext, 2, axis=0)[CONV_HALO:, :]
        conv = cw[0:1] * hc2 + cw[1:2] * hc1 + cw[2:3] * hc
        gb, za = _proj_pair(xg, win_ref, pair + 1)
        mix_ref[:, out_lanes] = (
            gb * r * conv * _silu_of_double(za * rh)).astype(_BF16)

    pair = 0
    for pos, (kind, j) in enumerate(EVEN_ORDER):
        out_lanes = slice(pos * SLAB, (pos + 1) * SLAB)
        if kind == "a":
            slab_a(j, pair, out_lanes)
            pair += 2
        else:
            slab_b(j, pair, out_lanes)
            pair += 1


def _odd_mix(xg, r, mix_ref, t, win_ref, lng_ref, lnb_ref, ws_ref, bias_ref):
    del t
    tm = xg.shape[0]
    n_chunks = tm // CHUNK
    n_slabs = D_MODEL // SLAB
    heads_per_slab = SLAB // C_HEAD_DIM

    pv = [p for pair in range(n_slabs // 2)
          for p in _proj_pair(xg, win_ref, pair)]
    mu = jnp.sum((pv[0] + pv[1]) + (pv[2] + pv[3]), axis=-1,
                 keepdims=True) * (1.0 / D_MODEL)
    vc = [p - mu for p in pv]
    sq = [c * c for c in vc]
    var = jnp.sum((sq[0] + sq[1]) + (sq[2] + sq[3]), axis=-1,
                  keepdims=True) * (1.0 / D_MODEL)
    rstd = r * lax.rsqrt(r * r * var + EPS)
    rh = 0.5 * r

    row = lax.broadcasted_iota(jnp.int32, (CHUNK, CHUNK), 0)
    col = lax.broadcasted_iota(jnp.int32, (CHUNK, CHUNK), 1)
    for j in range(n_slabs):
        lanes = slice(j * SLAB, (j + 1) * SLAB)
        vn = (vc[j] * rstd * lng_ref[:, lanes] + lnb_ref[:, lanes]).astype(_BF16)
        sv = []
        for k in range(heads_per_slab):
            hd = j * heads_per_slab + k
            ws = jnp.where(col <= row, ws_ref[hd], 0.0).astype(_BF16)
            hl = slice(k * C_HEAD_DIM, (k + 1) * C_HEAD_DIM)
            rhs = jnp.concatenate(
                [vn[n * CHUNK:(n + 1) * CHUNK, hl] for n in range(n_chunks)],
                axis=-1)
            out = _dot(ws, rhs)
            sv.append(jnp.concatenate(
                [out[:, n * C_HEAD_DIM:(n + 1) * C_HEAD_DIM]
                 for n in range(n_chunks)], axis=0))
        bias = bias_ref[:, lanes]
        sv = (jnp.concatenate(sv, axis=-1)
              + jnp.concatenate([bias] * n_chunks, axis=0))
        u, z = _proj_pair(xg, win_ref, n_slabs // 2 + j)
        mix_ref[:, lanes] = (
            u * r * sv * _silu_of_double(z * rh)).astype(_BF16)


def _layer_kernel(mix_fn, slab_cols, mix_rows, n_params, *refs):
    x_ref, pre_ref, post_ref, win_ref = refs[:4]
    params = refs[4:4 + n_params]
    (wout_ref, o_ref, mix_scr, win_scr,
     wout_scr) = refs[4 + n_params:9 + n_params]
    carries = refs[9 + n_params:]
    tm = x_ref.shape[1]
    t = pl.program_id(1)

    @pl.when(jnp.logical_and(pl.program_id(0) == 0, t == 0))
    def _():
        gain = pre_ref[...]
        for pos, col in enumerate(slab_cols):
            win_scr[:, pos * SLAB:(pos + 1) * SLAB] = (
                win_ref[:, col:col + SLAB] * gain).astype(_BF16)
        for pos, row in enumerate(mix_rows):
            wout_scr[pos * SLAB:(pos + 1) * SLAB, :] = (
                wout_ref[row:row + SLAB, :].astype(_BF16))

    x = x_ref[0]
    mix_fn(x.astype(_BF16), _row_scale(x), mix_scr, t, win_scr, *params,
           *carries)

    assert sum(POST_ROWS) == tm
    start = 0
    for n_rows in POST_ROWS:
        rows = slice(start, start + n_rows)
        start += n_rows
        m = _dot(mix_scr[rows, :], wout_scr[...])
        o_ref[0, rows, :] = (x_ref[0, rows, :]
                             + m * _row_scale(m) * post_ref[...])


def _full(shape):
    return pl.BlockSpec(shape, lambda b, t: (0,) * len(shape),
                        pipeline_mode=pl.Buffered(1))


def _layer_call(name, mix_fn, slab_cols, mix_rows, x, pre, post, w_in, params,
                w_out, carry_shapes):
    bsz, seq, d = x.shape
    assert sorted(slab_cols) == list(range(0, w_in.shape[1], SLAB))
    assert sorted(mix_rows) == list(range(0, w_out.shape[0], SLAB))
    row_spec = pl.BlockSpec((1, ROW_TILE, d), lambda b, t: (b, t, 0))
    operands = (x, pre.reshape(-1, 1), post.reshape(1, -1), w_in, *params,
                w_out)
    scratch = [pltpu.VMEM((ROW_TILE, d), _BF16),
               pltpu.VMEM(w_in.shape, _BF16),
               pltpu.VMEM(w_out.shape, _BF16)]
    scratch += [pltpu.VMEM(s, _F32) for s in carry_shapes]
    return pl.pallas_call(
        functools.partial(_layer_kernel, mix_fn, slab_cols, mix_rows,
                          len(params)),
        name=name,
        grid=(bsz, seq // ROW_TILE),
        in_specs=[row_spec] + [_full(p.shape) for p in operands[1:]],
        out_specs=row_spec,
        out_shape=jax.ShapeDtypeStruct(x.shape, x.dtype),
        scratch_shapes=scratch,
        compiler_params=pltpu.CompilerParams(
            dimension_semantics=("arbitrary", "arbitrary"),
            vmem_limit_bytes=VMEM_LIMIT_BYTES),
    )(*operands)


def _even_layer(x, pre, post, w_in, conv_w, pool_w, pool_scale, w_out):
    pw = pool_w.astype(_BF16)
    zero = jnp.zeros_like(pw[0])
    pool_bd = jnp.stack([
        jnp.block([[pw[2 * j], zero], [zero, pw[2 * j + 1]]])
        for j in range(B_WIDTH // SLAB)])
    params = (conv_w, pool_bd, pool_scale.reshape(1, -1))
    carries = ((CONV_HALO, A_WIDTH), (POOL_HALO, B_WIDTH))
    return _layer_call("even_layer", _even_mix, EVEN_SLAB_COLS, EVEN_MIX_ROWS,
                       x, pre, post, w_in, params, w_out, carries)


def _odd_layer(x, pre, post, w_in, ln_g, ln_b, w_s, b_s, w_out):
    bias = jnp.repeat(b_s.T, C_HEAD_DIM, axis=1)
    params = (ln_g.reshape(1, -1), ln_b.reshape(1, -1), w_s, bias)
    return _layer_call("odd_layer", _odd_mix, ODD_SLAB_COLS, ODD_MIX_ROWS,
                       x, pre, post, w_in, params, w_out, ())


def kernel(x, pre_norm, post_norm, even_w_in, even_conv_w, even_pool_w, even_pool_scale, even_w_out, odd_w_in, odd_ln_g, odd_ln_b, odd_w_s, odd_b_s, odd_w_out):
    depth = pre_norm.shape[0]
    for i in range(depth):
        j = i // 2
        if i % 2 == 0:
            x = _even_layer(x, pre_norm[i], post_norm[i], even_w_in[j],
                            even_conv_w[j], even_pool_w[j], even_pool_scale[j],
                            even_w_out[j])
        else:
            x = _odd_layer(x, pre_norm[i], post_norm[i], odd_w_in[j],
                           odd_ln_g[j], odd_ln_b[j], odd_w_s[j], odd_b_s[j],
                           odd_w_out[j])
    return x
```

```python
import functools

import jax
import jax.numpy as jnp
from jax import lax
from jax.experimental import pallas as pl
from jax.experimental.pallas import tpu as pltpu

D_MODEL = 1024
A_WIDTH = 512
B_WIDTH = 512
POOL_WINDOWS = (2, 4, 8, 16)
GROUP_DIM = B_WIDTH // len(POOL_WINDOWS)
C_HEADS = 8
C_HEAD_DIM = 128
CHUNK = 128
EPS = 1e-6

ROW_TILE = 1024
POST_ROWS = (1024,)
SLAB = 256
CONV_HALO = 8
POOL_HALO = 16
VMEM_LIMIT_BYTES = 56 * 1024 * 1024

_A, _B = A_WIDTH, B_WIDTH
EVEN_ORDER = (("a", 0), ("a", 1), ("b", 0), ("b", 1))


def _even_cols(kind, j):
    if kind == "a":
        return (0 * _A + j * SLAB, 2 * _A + j * SLAB,
                1 * _A + j * SLAB, 3 * _A + j * SLAB)
    return (4 * _A + j * SLAB, 4 * _A + _B + j * SLAB)


EVEN_SLAB_COLS = tuple(c for kind, j in EVEN_ORDER for c in _even_cols(kind, j))
ODD_SLAB_COLS = tuple(
    D_MODEL + j * SLAB for j in range(D_MODEL // SLAB)
) + tuple(
    col for j in range(D_MODEL // SLAB)
    for col in (j * SLAB, 2 * D_MODEL + j * SLAB))
EVEN_MIX_ROWS = tuple((0 if kind == "a" else _A) + j * SLAB
                      for kind, j in EVEN_ORDER)
ODD_MIX_ROWS = tuple(j * SLAB for j in range(D_MODEL // SLAB))

_F32 = jnp.float32
_BF16 = jnp.bfloat16


def _silu_of_double(half):
    return half + half * jnp.tanh(half)


def _dot(a, b):
    return jnp.dot(a, b, preferred_element_type=_F32)


def _row_scale(v):
    return lax.rsqrt(jnp.mean(v * v, axis=-1, keepdims=True) + EPS)


def _proj_pair(xg, win_ref, pair):
    out = _dot(xg, win_ref[:, 2 * pair * SLAB:(2 * pair + 2) * SLAB])
    return out[:, :SLAB], out[:, SLAB:]


def _even_mix(xg, r, mix_ref, t, win_ref, convw_ref, poolw_ref, pscale_ref,
              hc_carry, xp_carry):
    tm = xg.shape[0]
    first = t == 0
    groups_per_slab = SLAB // GROUP_DIM
    r2 = r * r
    rh = 0.5 * r
    pos1 = t * tm + lax.broadcasted_iota(jnp.int32, (tm, GROUP_DIM), 0) + 1
    inv_pos = 1.0 / pos1.astype(_F32)

    del poolw_ref

    def slab_b(j, pair, out_lanes):
        lanes = slice(j * SLAB, (j + 1) * SLAB)
        xp, zp = _proj_pair(xg, win_ref, pair)
        xp = xp * r
        carry = jnp.where(first, 0.0, xp_carry[:, lanes])
        xp_carry[:, lanes] = xp[tm - POOL_HALO:, :]
        acc = jnp.concatenate([carry, xp], axis=0)
        shift = 1
        for _ in range(j * groups_per_slab):
            acc = acc + pltpu.roll(acc, shift, axis=0)
            shift *= 2
        pooled = []
        for k in range(groups_per_slab):
            w = POOL_WINDOWS[j * groups_per_slab + k]
            acc = acc + pltpu.roll(acc, shift, axis=0)
            shift *= 2
            inv_count = jnp.where(pos1 < w, inv_pos, 1.0 / w)
            pooled.append(acc[POOL_HALO:, :GROUP_DIM] * inv_count
                          - xp[:, k * GROUP_DIM:(k + 1) * GROUP_DIM])
            acc = acc[:, GROUP_DIM:]
        mixed = jnp.concatenate(pooled, axis=-1)
        mix_ref[:, out_lanes] = (
            mixed * pscale_ref[:, lanes] * _silu_of_double(zp * rh)
        ).astype(_BF16)

    def slab_a(j, pair, out_lanes):
        lanes = slice(j * SLAB, (j + 1) * SLAB)
        xa, gc = _proj_pair(xg, win_ref, pair)
        cw = convw_ref[:, lanes]
        hc = gc * xa * r2
        carry = jnp.where(first, 0.0, hc_carry[:, lanes])
        hc_carry[:, lanes] = hc[tm - CONV_HALO:, :]
        ext = jnp.concatenate([carry, hc], axis=0)
        hc1 = pltpu.roll(ext, 1, axis=0)[CONV_HALO:, :]
        hc2 = pltpu.roll(ext, 2, axis=0)[CONV_HALO:, :]
        conv = cw[0:1] * hc2 + cw[1:2] * hc1 + cw[2:3] * hc
        gb, za = _proj_pair(xg, win_ref, pair + 1)
        mix_ref[:, out_lanes] = (
            gb * r * conv * _silu_of_double(za * rh)).astype(_BF16)

    pair = 0
    for pos, (kind, j) in enumerate(EVEN_ORDER):
        out_lanes = slice(pos * SLAB, (pos + 1) * SLAB)
        if kind == "a":
            slab_a(j, pair, out_lanes)
            pair += 2
        else:
            slab_b(j, pair, out_lanes)
            pair += 1


def _odd_mix(xg, r, mix_ref, t, win_ref, lng_ref, lnb_ref, ws_ref, bias_ref):
    del t
    tm = xg.shape[0]
    n_chunks = tm // CHUNK
    n_slabs = D_MODEL // SLAB
    heads_per_slab = SLAB // C_HEAD_DIM

    pv = [p for pair in range(n_slabs // 2)
          for p in _proj_pair(xg, win_ref, pair)]
    mu = jnp.sum((pv[0] + pv[1]) + (pv[2] + pv[3]), axis=-1,
                 keepdims=True) * (1.0 / D_MODEL)
    vc = [p - mu for p in pv]
    sq = [c * c for c in vc]
    var = jnp.sum((sq[0] + sq[1]) + (sq[2] + sq[3]), axis=-1,
                  keepdims=True) * (1.0 / D_MODEL)
    rstd = r * lax.rsqrt(r * r * var + EPS)
    rh = 0.5 * r

    row = lax.broadcasted_iota(jnp.int32, (CHUNK, CHUNK), 0)
    col = lax.broadcasted_iota(jnp.int32, (CHUNK, CHUNK), 1)
    for j in range(n_slabs):
        lanes = slice(j * SLAB, (j + 1) * SLAB)
        vn = (vc[j] * rstd * lng_ref[:, lanes] + lnb_ref[:, lanes]).astype(_BF16)
        sv = []
        for k in range(heads_per_slab):
            hd = j * heads_per_slab + k
            ws = jnp.where(col <= row, ws_ref[hd], 0.0).astype(_BF16)
            hl = slice(k * C_HEAD_DIM, (k + 1) * C_HEAD_DIM)
            rhs = jnp.concatenate(
                [vn[n * CHUNK:(n + 1) * CHUNK, hl] for n in range(n_chunks)],
                axis=-1)
            out = _dot(ws, rhs)
            sv.append(jnp.concatenate(
                [out[:, n * C_HEAD_DIM:(n + 1) * C_HEAD_DIM]
                 for n in range(n_chunks)], axis=0))
        bias = bias_ref[:, lanes]
        sv = (jnp.concatenate(sv, axis=-1)
              + jnp.concatenate([bias] * n_chunks, axis=0))
        u, z = _proj_pair(xg, win_ref, n_slabs // 2 + j)
        mix_ref[:, lanes] = (
            u * r * sv * _silu_of_double(z * rh)).astype(_BF16)


def _layer_kernel(mix_fn, fold_fn, slab_cols, mix_rows, n_params, *refs):
    x_ref, pre_ref, post_ref, win_ref = refs[:4]
    params = refs[4:4 + n_params]
    (wout_ref, o_ref, mix_scr, win_scr,
     wout_scr) = refs[4 + n_params:9 + n_params]
    carries = refs[9 + n_params:]
    tm = x_ref.shape[1]
    t = pl.program_id(1)

    @pl.when(jnp.logical_and(pl.program_id(0) == 0, t == 0))
    def _():
        for pos, col in enumerate(slab_cols):
            win_scr[:, pos * SLAB:(pos + 1) * SLAB] = fold_fn(
                col, win_ref[:, col:col + SLAB], *params).astype(_BF16)
        for pos, row in enumerate(mix_rows):
            wout_scr[pos * SLAB:(pos + 1) * SLAB, :] = (
                wout_ref[row:row + SLAB, :].astype(_BF16))

    x = x_ref[0]
    xg = (x * pre_ref[...]).astype(_BF16)
    mix_fn(xg, _row_scale(x), mix_scr, t, win_scr, *params, *carries)

    assert sum(POST_ROWS) == tm
    start = 0
    for n_rows in POST_ROWS:
        rows = slice(start, start + n_rows)
        start += n_rows
        m = _dot(mix_scr[rows, :], wout_scr[...])
        o_ref[0, rows, :] = (x_ref[0, rows, :]
                             + m * _row_scale(m) * post_ref[...])


def _full(shape):
    return pl.BlockSpec(shape, lambda b, t: (0,) * len(shape),
                        pipeline_mode=pl.Buffered(1))


def _layer_call(name, mix_fn, fold_fn, slab_cols, mix_rows, x, pre, post, w_in,
                params, w_out, carry_shapes):
    bsz, seq, d = x.shape
    assert sorted(slab_cols) == list(range(0, w_in.shape[1], SLAB))
    assert sorted(mix_rows) == list(range(0, w_out.shape[0], SLAB))
    row_spec = pl.BlockSpec((1, ROW_TILE, d), lambda b, t: (b, t, 0))
    operands = (x, pre.reshape(1, -1), post.reshape(1, -1), w_in, *params,
                w_out)
    scratch = [pltpu.VMEM((ROW_TILE, d), _BF16),
               pltpu.VMEM(w_in.shape, _BF16),
               pltpu.VMEM(w_out.shape, _BF16)]
    scratch += [pltpu.VMEM(s, _F32) for s in carry_shapes]
    return pl.pallas_call(
        functools.partial(_layer_kernel, mix_fn, fold_fn, slab_cols, mix_rows,
                          len(params)),
        name=name,
        grid=(bsz, seq // ROW_TILE),
        in_specs=[row_spec] + [_full(p.shape) for p in operands[1:]],
        out_specs=row_spec,
        out_shape=jax.ShapeDtypeStruct(x.shape, x.dtype),
        scratch_shapes=scratch,
        compiler_params=pltpu.CompilerParams(
            dimension_semantics=("arbitrary", "arbitrary"),
            vmem_limit_bytes=VMEM_LIMIT_BYTES),
    )(*operands)


def _even_fold(col, w, convw_ref, poolw_ref, pscale_ref):
    del convw_ref, pscale_ref
    j, rem = divmod(col - 4 * A_WIDTH, SLAB)
    if 0 <= j < B_WIDTH // SLAB and rem == 0:
        return jnp.dot(w, poolw_ref[j], preferred_element_type=_F32,
                       precision=lax.Precision.HIGHEST)
    return w


def _no_fold(col, w, *params):
    del col, params
    return w


def _even_layer(x, pre, post, w_in, conv_w, pool_w, pool_scale, w_out):
    zero = jnp.zeros_like(pool_w[0])
    pool_bd = jnp.stack([
        jnp.block([[pool_w[2 * j], zero], [zero, pool_w[2 * j + 1]]])
        for j in range(B_WIDTH // SLAB)])
    params = (conv_w, pool_bd, pool_scale.reshape(1, -1))
    carries = ((CONV_HALO, A_WIDTH), (POOL_HALO, B_WIDTH))
    return _layer_call("even_layer", _even_mix, _even_fold, EVEN_SLAB_COLS,
                       EVEN_MIX_ROWS, x, pre, post, w_in, params, w_out,
                       carries)


def _odd_layer(x, pre, post, w_in, ln_g, ln_b, w_s, b_s, w_out):
    bias = jnp.repeat(b_s.T, C_HEAD_DIM, axis=1)
    params = (ln_g.reshape(1, -1), ln_b.reshape(1, -1), w_s, bias)
    return _layer_call("odd_layer", _odd_mix, _no_fold, ODD_SLAB_COLS,
                       ODD_MIX_ROWS, x, pre, post, w_in, params, w_out, ())


def kernel(x, pre_norm, post_norm, even_w_in, even_conv_w, even_pool_w, even_pool_scale, even_w_out, odd_w_in, odd_ln_g, odd_ln_b, odd_w_s, odd_b_s, odd_w_out):
    depth = pre_norm.shape[0]
    for i in range(depth):
        j = i // 2
        if i % 2 == 0:
            x = _even_layer(x, pre_norm[i], post_norm[i], even_w_in[j],
                            even_conv_w[j], even_pool_w[j], even_pool_scale[j],
                            even_w_out[j])
        else:
            x = _odd_layer(x, pre_norm[i], post_norm[i], odd_w_in[j],
                           odd_ln_g[j], odd_ln_b[j], odd_w_s[j], odd_b_s[j],
                           odd_w_out[j])
    return x
```

```python
import functools

import jax
import jax.numpy as jnp
from jax import lax
from jax.experimental import pallas as pl
from jax.experimental.pallas import tpu as pltpu

D_MODEL = 1024
A_WIDTH = 512
B_WIDTH = 512
POOL_WINDOWS = (2, 4, 8, 16)
GROUP_DIM = B_WIDTH // len(POOL_WINDOWS)
C_HEADS = 8
C_HEAD_DIM = 128
CHUNK = 128
EPS = 1e-6

ROW_TILE = 1024
POST_ROWS = (1024,)
SLAB = 256
CONV_HALO = 8
POOL_HALO = 16
VMEM_LIMIT_BYTES = 56 * 1024 * 1024

_A, _B = A_WIDTH, B_WIDTH
EVEN_ORDER = (("a", 0), ("a", 1), ("b", 0), ("b", 1))


def _even_cols(kind, j):
    if kind == "a":
        return (0 * _A + j * SLAB, 2 * _A + j * SLAB,
                1 * _A + j * SLAB, 3 * _A + j * SLAB)
    return (4 * _A + j * SLAB, 4 * _A + _B + j * SLAB)


EVEN_SLAB_COLS = tuple(c for kind, j in EVEN_ORDER for c in _even_cols(kind, j))
ODD_SLAB_COLS = tuple(
    D_MODEL + j * SLAB for j in range(D_MODEL // SLAB)
) + tuple(
    col for j in range(D_MODEL // SLAB)
    for col in (j * SLAB, 2 * D_MODEL + j * SLAB))
EVEN_MIX_ROWS = tuple((0 if kind == "a" else _A) + j * SLAB
                      for kind, j in EVEN_ORDER)
ODD_MIX_ROWS = tuple(j * SLAB for j in range(D_MODEL // SLAB))
UZ_EARLY = 1

_F32 = jnp.float32
_BF16 = jnp.bfloat16


def _silu_of_double(half):
    return half + half * jnp.tanh(half)


def _dot(a, b):
    return jnp.dot(a, b, preferred_element_type=_F32)


def _row_scale(v):
    return lax.rsqrt(jnp.mean(v * v, axis=-1, keepdims=True) + EPS)


def _proj_pair(xg, win_ref, pair):
    out = _dot(xg, win_ref[:, 2 * pair * SLAB:(2 * pair + 2) * SLAB])
    return out[:, :SLAB], out[:, SLAB:]


def _even_mix(xg, r, mix_ref, t, win_ref, convw_ref, poolw_ref, pscale_ref,
              hc_carry, xp_carry):
    tm = xg.shape[0]
    first = t == 0
    groups_per_slab = SLAB // GROUP_DIM
    r2 = r * r
    rh = 0.5 * r
    pos1 = t * tm + lax.broadcasted_iota(jnp.int32, (tm, GROUP_DIM), 0) + 1
    inv_pos = 1.0 / pos1.astype(_F32)

    del poolw_ref

    def slab_b(j, pair, out_lanes):
        lanes = slice(j * SLAB, (j + 1) * SLAB)
        xp, zp = _proj_pair(xg, win_ref, pair)
        xp = xp * r
        carry = jnp.where(first, 0.0, xp_carry[:, lanes])
        xp_carry[:, lanes] = xp[tm - POOL_HALO:, :]
        acc = jnp.concatenate([carry, xp], axis=0)
        shift = 1
        for _ in range(j * groups_per_slab):
            acc = acc + pltpu.roll(acc, shift, axis=0)
            shift *= 2
        pooled = []
        for k in range(groups_per_slab):
            w = POOL_WINDOWS[j * groups_per_slab + k]
            acc = acc + pltpu.roll(acc, shift, axis=0)
            shift *= 2
            inv_count = jnp.where(pos1 < w, inv_pos, 1.0 / w)
            pooled.append(acc[POOL_HALO:, :GROUP_DIM] * inv_count
                          - xp[:, k * GROUP_DIM:(k + 1) * GROUP_DIM])
            acc = acc[:, GROUP_DIM:]
        mixed = jnp.concatenate(pooled, axis=-1)
        mix_ref[:, out_lanes] = (
            mixed * pscale_ref[:, lanes] * _silu_of_double(zp * rh)
        ).astype(_BF16)

    def slab_a(j, pair, out_lanes):
        lanes = slice(j * SLAB, (j + 1) * SLAB)
        xa, gc = _proj_pair(xg, win_ref, pair)
        cw = convw_ref[:, lanes]
        hc = gc * xa * r2
        carry = jnp.where(first, 0.0, hc_carry[:, lanes])
        hc_carry[:, lanes] = hc[tm - CONV_HALO:, :]
        ext = jnp.concatenate([carry, hc], axis=0)
        hc1 = pltpu.roll(ext, 1, axis=0)[CONV_HALO:, :]
        hc2 = pltpu.roll(ext, 2, axis=0)[CONV_HALO:, :]
        conv = cw[0:1] * hc2 + cw[1:2] * hc1 + cw[2:3] * hc
        gb, za = _proj_pair(xg, win_ref, pair + 1)
        mix_ref[:, out_lanes] = (
            gb * r * conv * _silu_of_double(za * rh)).astype(_BF16)

    pair = 0
    for pos, (kind, j) in enumerate(EVEN_ORDER):
        out_lanes = slice(pos * SLAB, (pos + 1) * SLAB)
        if kind == "a":
            slab_a(j, pair, out_lanes)
            pair += 2
        else:
            slab_b(j, pair, out_lanes)
            pair += 1


def _odd_mix(xg, r, mix_ref, t, win_ref, lng_ref, lnb_ref, ws_ref, bias_ref):
    del t
    tm = xg.shape[0]
    n_chunks = tm // CHUNK
    n_slabs = D_MODEL // SLAB
    heads_per_slab = SLAB // C_HEAD_DIM

    pv = [p for pair in range(n_slabs // 2)
          for p in _proj_pair(xg, win_ref, pair)]
    uz = [_proj_pair(xg, win_ref, n_slabs // 2 + j) for j in range(UZ_EARLY)]
    mu = jnp.sum((pv[0] + pv[1]) + (pv[2] + pv[3]), axis=-1,
                 keepdims=True) * (1.0 / D_MODEL)
    vc = [p - mu for p in pv]
    sq = [c * c for c in vc]
    var = jnp.sum((sq[0] + sq[1]) + (sq[2] + sq[3]), axis=-1,
                  keepdims=True) * (1.0 / D_MODEL)
    rstd = r * lax.rsqrt(r * r * var + EPS)
    rh = 0.5 * r

    row = lax.broadcasted_iota(jnp.int32, (CHUNK, CHUNK), 0)
    col = lax.broadcasted_iota(jnp.int32, (CHUNK, CHUNK), 1)
    svs = []
    for j in range(n_slabs):
        lanes = slice(j * SLAB, (j + 1) * SLAB)
        vn = (vc[j] * rstd * lng_ref[:, lanes] + lnb_ref[:, lanes]).astype(_BF16)
        sv = []
        for k in range(heads_per_slab):
            hd = j * heads_per_slab + k
            ws = jnp.where(col <= row, ws_ref[hd], 0.0).astype(_BF16)
            hl = slice(k * C_HEAD_DIM, (k + 1) * C_HEAD_DIM)
            rhs = jnp.concatenate(
                [vn[n * CHUNK:(n + 1) * CHUNK, hl] for n in range(n_chunks)],
                axis=-1)
            out = _dot(ws, rhs)
            sv.append(jnp.concatenate(
                [out[:, n * C_HEAD_DIM:(n + 1) * C_HEAD_DIM]
                 for n in range(n_chunks)], axis=0))
        bias = bias_ref[:, lanes]
        svs.append(jnp.concatenate(sv, axis=-1)
                   + jnp.concatenate([bias] * n_chunks, axis=0))
    uz += [_proj_pair(xg, win_ref, n_slabs // 2 + j)
           for j in range(UZ_EARLY, n_slabs)]
    for j in range(n_slabs):
        u, z = uz[j]
        mix_ref[:, j * SLAB:(j + 1) * SLAB] = (
            u * r * svs[j] * _silu_of_double(z * rh)).astype(_BF16)


def _layer_kernel(mix_fn, fold_fn, slab_cols, mix_rows, n_params, *refs):
    x_ref, pre_ref, post_ref, win_ref = refs[:4]
    params = refs[4:4 + n_params]
    (wout_ref, o_ref, mix_scr, win_scr,
     wout_scr) = refs[4 + n_params:9 + n_params]
    carries = refs[9 + n_params:]
    tm = x_ref.shape[1]
    t = pl.program_id(1)

    @pl.when(jnp.logical_and(pl.program_id(0) == 0, t == 0))
    def _():
        for pos, col in enumerate(slab_cols):
            win_scr[:, pos * SLAB:(pos + 1) * SLAB] = fold_fn(
                col, win_ref[:, col:col + SLAB], *params).astype(_BF16)
        for pos, row in enumerate(mix_rows):
            wout_scr[pos * SLAB:(pos + 1) * SLAB, :] = (
                wout_ref[row:row + SLAB, :].astype(_BF16))

    x = x_ref[0]
    xg = (x * pre_ref[...]).astype(_BF16)
    mix_fn(xg, _row_scale(x), mix_scr, t, win_scr, *params, *carries)

    assert sum(POST_ROWS) == tm
    start = 0
    for n_rows in POST_ROWS:
        rows = slice(start, start + n_rows)
        start += n_rows
        m = _dot(mix_scr[rows, :], wout_scr[...])
        o_ref[0, rows, :] = (x_ref[0, rows, :]
                             + m * _row_scale(m) * post_ref[...])


def _full(shape):
    return pl.BlockSpec(shape, lambda b, t: (0,) * len(shape),
                        pipeline_mode=pl.Buffered(1))


def _layer_call(name, mix_fn, fold_fn, slab_cols, mix_rows, x, pre, post, w_in,
                params, w_out, carry_shapes):
    bsz, seq, d = x.shape
    assert sorted(slab_cols) == list(range(0, w_in.shape[1], SLAB))
    assert sorted(mix_rows) == list(range(0, w_out.shape[0], SLAB))
    row_spec = pl.BlockSpec((1, ROW_TILE, d), lambda b, t: (b, t, 0))
    operands = (x, pre.reshape(1, -1), post.reshape(1, -1), w_in, *params,
                w_out)
    scratch = [pltpu.VMEM((ROW_TILE, d), _BF16),
               pltpu.VMEM(w_in.shape, _BF16),
               pltpu.VMEM(w_out.shape, _BF16)]
    scratch += [pltpu.VMEM(s, _F32) for s in carry_shapes]
    return pl.pallas_call(
        functools.partial(_layer_kernel, mix_fn, fold_fn, slab_cols, mix_rows,
                          len(params)),
        name=name,
        grid=(bsz, seq // ROW_TILE),
        in_specs=[row_spec] + [_full(p.shape) for p in operands[1:]],
        out_specs=row_spec,
        out_shape=jax.ShapeDtypeStruct(x.shape, x.dtype),
        scratch_shapes=scratch,
        compiler_params=pltpu.CompilerParams(
            dimension_semantics=("arbitrary", "arbitrary"),
            vmem_limit_bytes=VMEM_LIMIT_BYTES),
    )(*operands)


def _even_fold(col, w, convw_ref, poolw_ref, pscale_ref):
    del convw_ref, pscale_ref
    j, rem = divmod(col - 4 * A_WIDTH, SLAB)
    if 0 <= j < B_WIDTH // SLAB and rem == 0:
        return jnp.dot(w, poolw_ref[j], preferred_element_type=_F32,
                       precision=lax.Precision.HIGHEST)
    return w


def _no_fold(col, w, *params):
    del col, params
    return w


def _even_layer(x, pre, post, w_in, conv_w, pool_w, pool_scale, w_out):
    zero = jnp.zeros_like(pool_w[0])
    pool_bd = jnp.stack([
        jnp.block([[pool_w[2 * j], zero], [zero, pool_w[2 * j + 1]]])
        for j in range(B_WIDTH // SLAB)])
    params = (conv_w, pool_bd, pool_scale.reshape(1, -1))
    carries = ((CONV_HALO, A_WIDTH), (POOL_HALO, B_WIDTH))
    return _layer_call("even_layer", _even_mix, _even_fold, EVEN_SLAB_COLS,
                       EVEN_MIX_ROWS, x, pre, post, w_in, params, w_out,
                       carries)


def _odd_layer(x, pre, post, w_in, ln_g, ln_b, w_s, b_s, w_out):
    bias = jnp.repeat(b_s.T, C_HEAD_DIM, axis=1)
    params = (ln_g.reshape(1, -1), ln_b.reshape(1, -1), w_s, bias)
    return _layer_call("odd_layer", _odd_mix, _no_fold, ODD_SLAB_COLS,
                       ODD_MIX_ROWS, x, pre, post, w_in, params, w_out, ())


def kernel(x, pre_norm, post_norm, even_w_in, even_conv_w, even_pool_w, even_pool_scale, even_w_out, odd_w_in, odd_ln_g, odd_ln_b, odd_w_s, odd_b_s, odd_w_out):
    depth = pre_norm.shape[0]
    for i in range(depth):
        j = i // 2
        if i % 2 == 0:
            x = _even_layer(x, pre_norm[i], post_norm[i], even_w_in[j],
                            even_conv_w[j], even_pool_w[j], even_pool_scale[j],
                            even_w_out[j])
        else:
            x = _odd_layer(x, pre_norm[i], post_norm[i], odd_w_in[j],
                           odd_ln_g[j], odd_ln_b[j], odd_w_s[j], odd_b_s[j],
                           odd_w_out[j])
    return x
```

```python
import functools

import jax
import jax.numpy as jnp
from jax import lax
from jax.experimental import pallas as pl
from jax.experimental.pallas import tpu as pltpu

D_MODEL = 1024
A_WIDTH = 512
B_WIDTH = 512
POOL_WINDOWS = (2, 4, 8, 16)
GROUP_DIM = B_WIDTH // len(POOL_WINDOWS)
C_HEADS = 8
C_HEAD_DIM = 128
CHUNK = 128
EPS = 1e-6

ROW_TILE = 1024
POST_ROWS = (1024,)
SLAB = 256
CONV_HALO = 8
POOL_HALO = 16
VMEM_LIMIT_BYTES = 56 * 1024 * 1024

_A, _B = A_WIDTH, B_WIDTH
EVEN_ORDER = (("a", 0), ("b", 0), ("b", 1), ("a", 1))


def _even_cols(kind, j):
    if kind == "a":
        return (0 * _A + j * SLAB, 2 * _A + j * SLAB,
                1 * _A + j * SLAB, 3 * _A + j * SLAB)
    return (4 * _A + j * SLAB, 4 * _A + _B + j * SLAB)


EVEN_SLAB_COLS = tuple(c for kind, j in EVEN_ORDER for c in _even_cols(kind, j))
ODD_SLAB_COLS = tuple(
    D_MODEL + j * SLAB for j in range(D_MODEL // SLAB)
) + tuple(
    col for j in range(D_MODEL // SLAB)
    for col in (j * SLAB, 2 * D_MODEL + j * SLAB))
EVEN_MIX_ROWS = tuple((0 if kind == "a" else _A) + j * SLAB
                      for kind, j in EVEN_ORDER)
ODD_MIX_ROWS = tuple(j * SLAB for j in range(D_MODEL // SLAB))
UZ_EARLY = 1

_F32 = jnp.float32
_BF16 = jnp.bfloat16


def _silu_of_double(half):
    return half + half * jnp.tanh(half)


def _dot(a, b):
    return jnp.dot(a, b, preferred_element_type=_F32)


def _row_scale(v):
    return lax.rsqrt(jnp.mean(v * v, axis=-1, keepdims=True) + EPS)


def _proj_pair(xg, win_ref, pair):
    out = _dot(xg, win_ref[:, 2 * pair * SLAB:(2 * pair + 2) * SLAB])
    return out[:, :SLAB], out[:, SLAB:]


def _even_mix(xg, r, mix_ref, t, win_ref, convw_ref, poolw_ref, pscale_ref,
              hc_carry, xp_carry):
    tm = xg.shape[0]
    first = t == 0
    groups_per_slab = SLAB // GROUP_DIM
    r2 = r * r
    rh = 0.5 * r
    pos1 = t * tm + lax.broadcasted_iota(jnp.int32, (tm, GROUP_DIM), 0) + 1
    inv_pos = 1.0 / pos1.astype(_F32)

    del poolw_ref

    def slab_b(j, pair, out_lanes):
        lanes = slice(j * SLAB, (j + 1) * SLAB)
        xp, zp = _proj_pair(xg, win_ref, pair)
        xp = xp * r
        carry = jnp.where(first, 0.0, xp_carry[:, lanes])
        xp_carry[:, lanes] = xp[tm - POOL_HALO:, :]
        acc = jnp.concatenate([carry, xp], axis=0)
        shift = 1
        for _ in range(j * groups_per_slab):
            acc = acc + pltpu.roll(acc, shift, axis=0)
            shift *= 2
        pooled = []
        for k in range(groups_per_slab):
            w = POOL_WINDOWS[j * groups_per_slab + k]
            acc = acc + pltpu.roll(acc, shift, axis=0)
            shift *= 2
            inv_count = jnp.where(pos1 < w, inv_pos, 1.0 / w)
            pooled.append(acc[POOL_HALO:, :GROUP_DIM] * inv_count
                          - xp[:, k * GROUP_DIM:(k + 1) * GROUP_DIM])
            acc = acc[:, GROUP_DIM:]
        mixed = jnp.concatenate(pooled, axis=-1)
        mix_ref[:, out_lanes] = (
            mixed * pscale_ref[:, lanes] * _silu_of_double(zp * rh)
        ).astype(_BF16)

    def slab_a(j, pair, out_lanes):
        lanes = slice(j * SLAB, (j + 1) * SLAB)
        xa, gc = _proj_pair(xg, win_ref, pair)
        cw = convw_ref[:, lanes]
        hc = gc * xa * r2
        carry = jnp.where(first, 0.0, hc_carry[:, lanes])
        hc_carry[:, lanes] = hc[tm - CONV_HALO:, :]
        ext = jnp.concatenate([carry, hc], axis=0)
        hc1 = pltpu.roll(ext, 1, axis=0)[CONV_HALO:, :]
        hc2 = pltpu.roll(ext, 2, axis=0)[CONV_HALO:, :]
        conv = cw[0:1] * hc2 + cw[1:2] * hc1 + cw[2:3] * hc
        gb, za = _proj_pair(xg, win_ref, pair + 1)
        mix_ref[:, out_lanes] = (
            gb * r * conv * _silu_of_double(za * rh)).astype(_BF16)

    pair = 0
    for pos, (kind, j) in enumerate(EVEN_ORDER):
        out_lanes = slice(pos * SLAB, (pos + 1) * SLAB)
        if kind == "a":
            slab_a(j, pair, out_lanes)
            pair += 2
        else:
            slab_b(j, pair, out_lanes)
            pair += 1


def _odd_mix(xg, r, mix_ref, t, win_ref, lng_ref, lnb_ref, ws_ref, bias_ref):
    del t
    tm = xg.shape[0]
    n_chunks = tm // CHUNK
    n_slabs = D_MODEL // SLAB
    heads_per_slab = SLAB // C_HEAD_DIM

    pv = [p for pair in range(n_slabs // 2)
          for p in _proj_pair(xg, win_ref, pair)]
    uz = [_proj_pair(xg, win_ref, n_slabs // 2 + j) for j in range(UZ_EARLY)]
    mu = jnp.sum((pv[0] + pv[1]) + (pv[2] + pv[3]), axis=-1,
                 keepdims=True) * (1.0 / D_MODEL)
    vc = [p - mu for p in pv]
    sq = [c * c for c in vc]
    var = jnp.sum((sq[0] + sq[1]) + (sq[2] + sq[3]), axis=-1,
                  keepdims=True) * (1.0 / D_MODEL)
    rstd = r * lax.rsqrt(r * r * var + EPS)
    rh = 0.5 * r

    row = lax.broadcasted_iota(jnp.int32, (CHUNK, CHUNK), 0)
    col = lax.broadcasted_iota(jnp.int32, (CHUNK, CHUNK), 1)
    svs = []
    for j in range(n_slabs):
        lanes = slice(j * SLAB, (j + 1) * SLAB)
        vn = (vc[j] * rstd * lng_ref[:, lanes] + lnb_ref[:, lanes]).astype(_BF16)
        sv = []
        for k in range(heads_per_slab):
            hd = j * heads_per_slab + k
            ws = jnp.where(col <= row, ws_ref[hd], 0.0).astype(_BF16)
            hl = slice(k * C_HEAD_DIM, (k + 1) * C_HEAD_DIM)
            rhs = jnp.concatenate(
                [vn[n * CHUNK:(n + 1) * CHUNK, hl] for n in range(n_chunks)],
                axis=-1)
            out = _dot(ws, rhs)
            sv.append(jnp.concatenate(
                [out[:, n * C_HEAD_DIM:(n + 1) * C_HEAD_DIM]
                 for n in range(n_chunks)], axis=0))
        bias = bias_ref[:, lanes]
        svs.append(jnp.concatenate(sv, axis=-1)
                   + jnp.concatenate([bias] * n_chunks, axis=0))
    uz += [_proj_pair(xg, win_ref, n_slabs // 2 + j)
           for j in range(UZ_EARLY, n_slabs)]
    for j in range(n_slabs):
        u, z = uz[j]
        mix_ref[:, j * SLAB:(j + 1) * SLAB] = (
            u * r * svs[j] * _silu_of_double(z * rh)).astype(_BF16)


def _layer_kernel(mix_fn, fold_fn, slab_cols, mix_rows, n_params, *refs):
    x_ref, pre_ref, post_ref, win_ref = refs[:4]
    params = refs[4:4 + n_params]
    (wout_ref, o_ref, mix_scr, win_scr,
     wout_scr) = refs[4 + n_params:9 + n_params]
    carries = refs[9 + n_params:]
    tm = x_ref.shape[1]
    t = pl.program_id(1)

    @pl.when(jnp.logical_and(pl.program_id(0) == 0, t == 0))
    def _():
        for pos, col in enumerate(slab_cols):
            win_scr[:, pos * SLAB:(pos + 1) * SLAB] = fold_fn(
                col, win_ref[:, col:col + SLAB], *params).astype(_BF16)
        for pos, row in enumerate(mix_rows):
            wout_scr[pos * SLAB:(pos + 1) * SLAB, :] = (
                wout_ref[row:row + SLAB, :].astype(_BF16))

    x = x_ref[0]
    xg = (x * pre_ref[...]).astype(_BF16)
    mix_fn(xg, _row_scale(x), mix_scr, t, win_scr, *params, *carries)

    assert sum(POST_ROWS) == tm
    start = 0
    for n_rows in POST_ROWS:
        rows = slice(start, start + n_rows)
        start += n_rows
        m = _dot(mix_scr[rows, :], wout_scr[...])
        o_ref[0, rows, :] = (x_ref[0, rows, :]
                             + m * _row_scale(m) * post_ref[...])


def _full(shape):
    return pl.BlockSpec(shape, lambda b, t: (0,) * len(shape),
                        pipeline_mode=pl.Buffered(1))


def _layer_call(name, mix_fn, fold_fn, slab_cols, mix_rows, x, pre, post, w_in,
                params, w_out, carry_shapes):
    bsz, seq, d = x.shape
    assert sorted(slab_cols) == list(range(0, w_in.shape[1], SLAB))
    assert sorted(mix_rows) == list(range(0, w_out.shape[0], SLAB))
    row_spec = pl.BlockSpec((1, ROW_TILE, d), lambda b, t: (b, t, 0))
    operands = (x, pre.reshape(1, -1), post.reshape(1, -1), w_in, *params,
                w_out)
    scratch = [pltpu.VMEM((ROW_TILE, d), _BF16),
               pltpu.VMEM(w_in.shape, _BF16),
               pltpu.VMEM(w_out.shape, _BF16)]
    scratch += [pltpu.VMEM(s, _F32) for s in carry_shapes]
    return pl.pallas_call(
        functools.partial(_layer_kernel, mix_fn, fold_fn, slab_cols, mix_rows,
                          len(params)),
        name=name,
        grid=(bsz, seq // ROW_TILE),
        in_specs=[row_spec] + [_full(p.shape) for p in operands[1:]],
        out_specs=row_spec,
        out_shape=jax.ShapeDtypeStruct(x.shape, x.dtype),
        scratch_shapes=scratch,
        compiler_params=pltpu.CompilerParams(
            dimension_semantics=("arbitrary", "arbitrary"),
            vmem_limit_bytes=VMEM_LIMIT_BYTES),
    )(*operands)


def _even_fold(col, w, convw_ref, poolw_ref, pscale_ref):
    del convw_ref, pscale_ref
    j, rem = divmod(col - 4 * A_WIDTH, SLAB)
    if 0 <= j < B_WIDTH // SLAB and rem == 0:
        return jnp.dot(w, poolw_ref[j], preferred_element_type=_F32,
                       precision=lax.Precision.HIGHEST)
    return w


def _no_fold(col, w, *params):
    del col, params
    return w


def _even_layer(x, pre, post, w_in, conv_w, pool_w, pool_scale, w_out):
    zero = jnp.zeros_like(pool_w[0])
    pool_bd = jnp.stack([
        jnp.block([[pool_w[2 * j], zero], [zero, pool_w[2 * j + 1]]])
        for j in range(B_WIDTH // SLAB)])
    params = (conv_w, pool_bd, pool_scale.reshape(1, -1))
    carries = ((CONV_HALO, A_WIDTH), (POOL_HALO, B_WIDTH))
    return _layer_call("even_layer", _even_mix, _even_fold, EVEN_SLAB_COLS,
                       EVEN_MIX_ROWS, x, pre, post, w_in, params, w_out,
                       carries)


def _odd_layer(x, pre, post, w_in, ln_g, ln_b, w_s, b_s, w_out):
    bias = jnp.repeat(b_s.T, C_HEAD_DIM, axis=1)
    params = (ln_g.reshape(1, -1), ln_b.reshape(1, -1), w_s, bias)
    return _layer_call("odd_layer", _odd_mix, _no_fold, ODD_SLAB_COLS,
                       ODD_MIX_ROWS, x, pre, post, w_in, params, w_out, ())


def kernel(x, pre_norm, post_norm, even_w_in, even_conv_w, even_pool_w, even_pool_scale, even_w_out, odd_w_in, odd_ln_g, odd_ln_b, odd_w_s, odd_b_s, odd_w_out):
    depth = pre_norm.shape[0]
    for i in range(depth):
        j = i // 2
        if i % 2 == 0:
            x = _even_layer(x, pre_norm[i], post_norm[i], even_w_in[j],
                            even_conv_w[j], even_pool_w[j], even_pool_scale[j],
                            even_w_out[j])
        else:
            x = _odd_layer(x, pre_norm[i], post_norm[i], odd_w_in[j],
                           odd_ln_g[j], odd_ln_b[j], odd_w_s[j], odd_b_s[j],
                           odd_w_out[j])
    return x
```

```python
import functools

import jax
import jax.numpy as jnp
from jax import lax
from jax.experimental import pallas as pl
from jax.experimental.pallas import tpu as pltpu

D_MODEL = 1024
A_WIDTH = 512
B_WIDTH = 512
POOL_WINDOWS = (2, 4, 8, 16)
GROUP_DIM = B_WIDTH // len(POOL_WINDOWS)
C_HEADS = 8
C_HEAD_DIM = 128
CHUNK = 128
EPS = 1e-6

ROW_TILE = 1024
SLAB = 256
CONV_HALO = 8
POOL_HALO = 16
STAGE_ROWS = 256
VMEM_LIMIT_BYTES = 56 * 1024 * 1024

_A, _B = A_WIDTH, B_WIDTH
EVEN_ORDER = (("a", 0), ("a", 1), ("b", 0), ("b", 1))


def _even_cols(kind, j):
    if kind == "a":
        return (0 * _A + j * SLAB, 2 * _A + j * SLAB,
                1 * _A + j * SLAB, 3 * _A + j * SLAB)
    return (4 * _A + j * SLAB, 4 * _A + _B + j * SLAB)


EVEN_SLAB_COLS = tuple(c for kind, j in EVEN_ORDER for c in _even_cols(kind, j))
ODD_SLAB_COLS = tuple(
    D_MODEL + j * SLAB for j in range(D_MODEL // SLAB)
) + tuple(
    col for j in range(D_MODEL // SLAB)
    for col in (j * SLAB, 2 * D_MODEL + j * SLAB))
EVEN_MIX_ROWS = tuple((0 if kind == "a" else _A) + j * SLAB
                      for kind, j in EVEN_ORDER)
ODD_MIX_ROWS = tuple(j * SLAB for j in range(D_MODEL // SLAB))
UZ_EARLY = 1

_F32 = jnp.float32
_BF16 = jnp.bfloat16


def _silu_of_double(half):
    return half + half * jnp.tanh(half)


def _dot(a, b):
    return jnp.dot(a, b, preferred_element_type=_F32)


def _row_scale(v):
    return lax.rsqrt(jnp.mean(v * v, axis=-1, keepdims=True) + EPS)


def _proj_pair(xg, win_ref, pair):
    out = _dot(xg, win_ref[:, 2 * pair * SLAB:(2 * pair + 2) * SLAB])
    return out[:, :SLAB], out[:, SLAB:]


def _even_mix(xg, r, mix_ref, t, win_ref, convw_ref, poolw_ref, pscale_ref,
              hc_carry, xp_carry):
    tm = xg.shape[0]
    first = t == 0
    groups_per_slab = SLAB // GROUP_DIM
    r2 = r * r
    rh = 0.5 * r
    pos1 = t * tm + lax.broadcasted_iota(jnp.int32, (tm, GROUP_DIM), 0) + 1
    inv_pos = 1.0 / pos1.astype(_F32)

    del poolw_ref

    def slab_b(j, pair, out_lanes):
        lanes = slice(j * SLAB, (j + 1) * SLAB)
        xp, zp = _proj_pair(xg, win_ref, pair)
        xp = xp * r
        carry = jnp.where(first, 0.0, xp_carry[:, lanes])
        xp_carry[:, lanes] = xp[tm - POOL_HALO:, :]
        acc = jnp.concatenate([carry, xp], axis=0)
        shift = 1
        for _ in range(j * groups_per_slab):
            acc = acc + pltpu.roll(acc, shift, axis=0)
            shift *= 2
        pooled = []
        for k in range(groups_per_slab):
            w = POOL_WINDOWS[j * groups_per_slab + k]
            acc = acc + pltpu.roll(acc, shift, axis=0)
            shift *= 2
            inv_count = jnp.where(pos1 < w, inv_pos, 1.0 / w)
            pooled.append(acc[POOL_HALO:, :GROUP_DIM] * inv_count
                          - xp[:, k * GROUP_DIM:(k + 1) * GROUP_DIM])
            acc = acc[:, GROUP_DIM:]
        mixed = jnp.concatenate(pooled, axis=-1)
        mix_ref[:, out_lanes] = (
            mixed * pscale_ref[:, lanes] * _silu_of_double(zp * rh)
        ).astype(_BF16)

    def slab_a(j, pair, out_lanes):
        lanes = slice(j * SLAB, (j + 1) * SLAB)
        xa, gc = _proj_pair(xg, win_ref, pair)
        cw = convw_ref[:, lanes]
        hc = gc * xa * r2
        carry = jnp.where(first, 0.0, hc_carry[:, lanes])
        hc_carry[:, lanes] = hc[tm - CONV_HALO:, :]
        ext = jnp.concatenate([carry, hc], axis=0)
        hc1 = pltpu.roll(ext, 1, axis=0)[CONV_HALO:, :]
        hc2 = pltpu.roll(ext, 2, axis=0)[CONV_HALO:, :]
        conv = cw[0:1] * hc2 + cw[1:2] * hc1 + cw[2:3] * hc
        gb, za = _proj_pair(xg, win_ref, pair + 1)
        mix_ref[:, out_lanes] = (
            gb * r * conv * _silu_of_double(za * rh)).astype(_BF16)

    pair = 0
    for pos, (kind, j) in enumerate(EVEN_ORDER):
        out_lanes = slice(pos * SLAB, (pos + 1) * SLAB)
        if kind == "a":
            slab_a(j, pair, out_lanes)
            pair += 2
        else:
            slab_b(j, pair, out_lanes)
            pair += 1


def _odd_mix(xg, r, mix_ref, win_ref, lng_ref, lnb_ref, ws_ref, bias_ref):
    tm = xg.shape[0]
    n_chunks = tm // CHUNK
    n_slabs = D_MODEL // SLAB
    heads_per_slab = SLAB // C_HEAD_DIM

    pv = [p for pair in range(n_slabs // 2)
          for p in _proj_pair(xg, win_ref, pair)]
    uz = [_proj_pair(xg, win_ref, n_slabs // 2 + j) for j in range(UZ_EARLY)]
    mu = jnp.sum((pv[0] + pv[1]) + (pv[2] + pv[3]), axis=-1,
                 keepdims=True) * (1.0 / D_MODEL)
    vc = [p - mu for p in pv]
    sq = [c * c for c in vc]
    var = jnp.sum((sq[0] + sq[1]) + (sq[2] + sq[3]), axis=-1,
                  keepdims=True) * (1.0 / D_MODEL)
    rstd = r * lax.rsqrt(r * r * var + EPS)
    rh = 0.5 * r

    row = lax.broadcasted_iota(jnp.int32, (CHUNK, CHUNK), 0)
    col = lax.broadcasted_iota(jnp.int32, (CHUNK, CHUNK), 1)
    svs = []
    for j in range(n_slabs):
        lanes = slice(j * SLAB, (j + 1) * SLAB)
        vn = (vc[j] * rstd * lng_ref[:, lanes] + lnb_ref[:, lanes]).astype(_BF16)
        sv = []
        for k in range(heads_per_slab):
            hd = j * heads_per_slab + k
            ws = jnp.where(col <= row, ws_ref[hd], 0.0).astype(_BF16)
            hl = slice(k * C_HEAD_DIM, (k + 1) * C_HEAD_DIM)
            rhs = jnp.concatenate(
                [vn[n * CHUNK:(n + 1) * CHUNK, hl] for n in range(n_chunks)],
                axis=-1)
            out = _dot(ws, rhs)
            sv.append(jnp.concatenate(
                [out[:, n * C_HEAD_DIM:(n + 1) * C_HEAD_DIM]
                 for n in range(n_chunks)], axis=0))
        bias = bias_ref[:, lanes]
        svs.append(jnp.concatenate(sv, axis=-1)
                   + jnp.concatenate([bias] * n_chunks, axis=0))
    uz += [_proj_pair(xg, win_ref, n_slabs // 2 + j)
           for j in range(UZ_EARLY, n_slabs)]
    for j in range(n_slabs):
        u, z = uz[j]
        mix_ref[:, j * SLAB:(j + 1) * SLAB] = (
            u * r * svs[j] * _silu_of_double(z * rh)).astype(_BF16)


def _stage_weights(w_hbm, stage_ref, sem, store_chunk):
    n_rows, n_cols = w_hbm.shape
    stage = stage_ref.at[:, :n_cols] if stage_ref.shape[1] != n_cols else stage_ref
    for c in range(n_rows // STAGE_ROWS):
        copy = pltpu.make_async_copy(
            w_hbm.at[pl.ds(c * STAGE_ROWS, STAGE_ROWS), :], stage, sem)
        copy.start()
        copy.wait()
        store_chunk(c * STAGE_ROWS, stage)


def _even_fold(col, w, poolw_ref):
    j, rem = divmod(col - 4 * A_WIDTH, SLAB)
    if 0 <= j < B_WIDTH // SLAB and rem == 0:
        return jnp.dot(w, poolw_ref[j], preferred_element_type=_F32,
                       precision=lax.Precision.HIGHEST)
    return w


def _trunk_kernel(x_ref, pre_ref, post_ref, convw_ref, poolw_ref, pscale_ref,
                  lng_ref, lnb_ref, ws_ref, bias_ref,
                  win0_hbm, wout0_hbm, win1_hbm, wout1_hbm,
                  o_ref,
                  mix_scr, x1_scr, win_scr, wout_scr, stage_in, stage_out, sem,
                  hc_carry, xp_carry):
    t = pl.program_id(1)

    @pl.when(jnp.logical_and(pl.program_id(0) == 0, t == 0))
    def _():
        def store_in(layer, slab_cols, fold):
            def store(row0, stage):
                for pos, col in enumerate(slab_cols):
                    win_scr[layer, row0:row0 + STAGE_ROWS,
                            pos * SLAB:(pos + 1) * SLAB] = fold(
                                col, stage[:, col:col + SLAB]).astype(_BF16)
            return store

        def store_out(layer, mix_rows):
            def store(row0, stage):
                pos = mix_rows.index(row0 - row0 % SLAB)
                dst = pos * SLAB + row0 % SLAB
                wout_scr[layer, dst:dst + STAGE_ROWS, :] = stage[...].astype(_BF16)
            return store

        _stage_weights(win0_hbm, stage_in, sem, store_in(
            0, EVEN_SLAB_COLS, lambda col, w: _even_fold(col, w, poolw_ref)))
        _stage_weights(wout0_hbm, stage_out, sem, store_out(0, EVEN_MIX_ROWS))
        _stage_weights(win1_hbm, stage_in, sem, store_in(
            1, ODD_SLAB_COLS, lambda col, w: w))
        _stage_weights(wout1_hbm, stage_out, sem, store_out(1, ODD_MIX_ROWS))

    def post(layer, x):
        m = _dot(mix_scr[...], wout_scr[layer])
        return x + m * _row_scale(m) * post_ref[layer]

    x = x_ref[0]
    xg = (x * pre_ref[0]).astype(_BF16)
    _even_mix(xg, _row_scale(x), mix_scr, t, win_scr.at[0], convw_ref,
              poolw_ref, pscale_ref, hc_carry, xp_carry)
    x1_scr[...] = post(0, x)

    x1 = x1_scr[...]
    xg = (x1 * pre_ref[1]).astype(_BF16)
    _odd_mix(xg, _row_scale(x1), mix_scr, win_scr.at[1], lng_ref, lnb_ref,
             ws_ref, bias_ref)
    o_ref[0] = post(1, x1_scr[...])


def _full(shape):
    return pl.BlockSpec(shape, lambda b, t: (0,) * len(shape),
                        pipeline_mode=pl.Buffered(1))


def kernel(x, pre_norm, post_norm, even_w_in, even_conv_w, even_pool_w, even_pool_scale, even_w_out, odd_w_in, odd_ln_g, odd_ln_b, odd_w_s, odd_b_s, odd_w_out):
    bsz, seq, d = x.shape
    assert pre_norm.shape[0] == 2 and STAGE_ROWS == SLAB
    pool_w = even_pool_w[0]
    zero = jnp.zeros_like(pool_w[0])
    pool_bd = jnp.stack([
        jnp.block([[pool_w[2 * j], zero], [zero, pool_w[2 * j + 1]]])
        for j in range(B_WIDTH // SLAB)])
    bias = jnp.repeat(odd_b_s[0].T, C_HEAD_DIM, axis=1)
    small = (pre_norm.reshape(2, 1, d), post_norm.reshape(2, 1, d),
             even_conv_w[0], pool_bd, even_pool_scale[0].reshape(1, -1),
             odd_ln_g[0].reshape(1, -1), odd_ln_b[0].reshape(1, -1),
             odd_w_s[0], bias)
    weights = (even_w_in[0], even_w_out[0], odd_w_in[0], odd_w_out[0])
    row_spec = pl.BlockSpec((1, ROW_TILE, d), lambda b, t: (b, t, 0))
    scratch = [
        pltpu.VMEM((ROW_TILE, d), _BF16),
        pltpu.VMEM((ROW_TILE, d), _F32),
        pltpu.VMEM((2,) + even_w_in.shape[1:], _BF16),
        pltpu.VMEM((2,) + even_w_out.shape[1:], _BF16),
        pltpu.VMEM((STAGE_ROWS, even_w_in.shape[2]), _F32),
        pltpu.VMEM((STAGE_ROWS, even_w_out.shape[2]), _F32),
        pltpu.SemaphoreType.DMA,
        pltpu.VMEM((CONV_HALO, A_WIDTH), _F32),
        pltpu.VMEM((POOL_HALO, B_WIDTH), _F32),
    ]
    return pl.pallas_call(
        _trunk_kernel,
        name="trunk",
        grid=(bsz, seq // ROW_TILE),
        in_specs=([row_spec] + [_full(p.shape) for p in small]
                  + [pl.BlockSpec(memory_space=pl.ANY)] * len(weights)),
        out_specs=row_spec,
        out_shape=jax.ShapeDtypeStruct(x.shape, x.dtype),
        scratch_shapes=scratch,
        compiler_params=pltpu.CompilerParams(
            dimension_semantics=("arbitrary", "arbitrary"),
            vmem_limit_bytes=VMEM_LIMIT_BYTES),
    )(x, *small, *weights)
```

```python
import functools

import jax
import jax.numpy as jnp
from jax import lax
from jax.experimental import pallas as pl
from jax.experimental.pallas import tpu as pltpu

D_MODEL = 1024
A_WIDTH = 512
B_WIDTH = 512
POOL_WINDOWS = (2, 4, 8, 16)
GROUP_DIM = B_WIDTH // len(POOL_WINDOWS)
C_HEADS = 8
C_HEAD_DIM = 128
CHUNK = 128
EPS = 1e-6

ROW_TILE = 1024
POST_ROWS = (1024,)
SLAB = 256
CONV_HALO = 8
POOL_HALO = 16
VMEM_LIMIT_BYTES = 56 * 1024 * 1024

_A, _B = A_WIDTH, B_WIDTH
EVEN_ORDER = (("a", 0), ("a", 1), ("b", 0), ("b", 1))


def _even_cols(kind, j):
    if kind == "a":
        return (0 * _A + j * SLAB, 2 * _A + j * SLAB,
                1 * _A + j * SLAB, 3 * _A + j * SLAB)
    return (4 * _A + j * SLAB, 4 * _A + _B + j * SLAB)


EVEN_SLAB_COLS = tuple(c for kind, j in EVEN_ORDER for c in _even_cols(kind, j))
ODD_SLAB_COLS = tuple(
    D_MODEL + j * SLAB for j in range(D_MODEL // SLAB)
) + tuple(
    col for j in range(D_MODEL // SLAB)
    for col in (j * SLAB, 2 * D_MODEL + j * SLAB))
EVEN_MIX_ROWS = tuple((0 if kind == "a" else _A) + j * SLAB
                      for kind, j in EVEN_ORDER)
ODD_MIX_ROWS = tuple(j * SLAB for j in range(D_MODEL // SLAB))
UZ_EARLY = 1

_F32 = jnp.float32
_BF16 = jnp.bfloat16


def _silu_of_double(half):
    return half + half * jnp.tanh(half)


def _dot(a, b):
    return jnp.dot(a, b, preferred_element_type=_F32)


def _row_scale(v):
    return lax.rsqrt(jnp.mean(v * v, axis=-1, keepdims=True) + EPS)


def _proj_pair(xg, win_ref, pair):
    out = _dot(xg, win_ref[:, 2 * pair * SLAB:(2 * pair + 2) * SLAB])
    return out[:, :SLAB], out[:, SLAB:]


def _even_mix(xg, r, mix_ref, t, win_ref, convw_ref, poolw_ref, pscale_ref,
              hc_carry, xp_carry):
    tm = xg.shape[0]
    first = t == 0
    groups_per_slab = SLAB // GROUP_DIM
    r2 = r * r
    rh = 0.5 * r
    pos1 = t * tm + lax.broadcasted_iota(jnp.int32, (tm, GROUP_DIM), 0) + 1
    inv_pos = 1.0 / pos1.astype(_F32)

    del poolw_ref

    def slab_b(j, pair, out_lanes):
        lanes = slice(j * SLAB, (j + 1) * SLAB)
        xp, zp = _proj_pair(xg, win_ref, pair)
        xp = xp * r
        carry = jnp.where(first, 0.0, xp_carry[:, lanes])
        xp_carry[:, lanes] = xp[tm - POOL_HALO:, :]
        acc = jnp.concatenate([carry, xp], axis=0)
        shift = 1
        for _ in range(j * groups_per_slab):
            acc = acc + pltpu.roll(acc, shift, axis=0)
            shift *= 2
        pooled = []
        for k in range(groups_per_slab):
            w = POOL_WINDOWS[j * groups_per_slab + k]
            acc = acc + pltpu.roll(acc, shift, axis=0)
            shift *= 2
            inv_count = jnp.where(pos1 < w, inv_pos, 1.0 / w)
            pooled.append(acc[POOL_HALO:, :GROUP_DIM] * inv_count
                          - xp[:, k * GROUP_DIM:(k + 1) * GROUP_DIM])
            acc = acc[:, GROUP_DIM:]
        mixed = jnp.concatenate(pooled, axis=-1)
        mix_ref[:, out_lanes] = (
            mixed * pscale_ref[:, lanes] * _silu_of_double(zp * rh)
        ).astype(_BF16)

    def slab_a(j, pair, out_lanes):
        lanes = slice(j * SLAB, (j + 1) * SLAB)
        xa, gc = _proj_pair(xg, win_ref, pair)
        cw = convw_ref[:, lanes]
        hc = gc * xa * r2
        carry = jnp.where(first, 0.0, hc_carry[:, lanes])
        hc_carry[:, lanes] = hc[tm - CONV_HALO:, :]
        ext = jnp.concatenate([carry, hc], axis=0)
        hc1 = pltpu.roll(ext, 1, axis=0)[CONV_HALO:, :]
        hc2 = pltpu.roll(ext, 2, axis=0)[CONV_HALO:, :]
        conv = cw[0:1] * hc2 + cw[1:2] * hc1 + cw[2:3] * hc
        gb, za = _proj_pair(xg, win_ref, pair + 1)
        mix_ref[:, out_lanes] = (
            gb * r * conv * _silu_of_double(za * rh)).astype(_BF16)

    pair = 0
    for pos, (kind, j) in enumerate(EVEN_ORDER):
        out_lanes = slice(pos * SLAB, (pos + 1) * SLAB)
        if kind == "a":
            slab_a(j, pair, out_lanes)
            pair += 2
        else:
            slab_b(j, pair, out_lanes)
            pair += 1


def _odd_mix(xg, r, mix_ref, t, win_ref, lng_ref, lnb_ref, ws_ref, bias_ref):
    del t
    tm = xg.shape[0]
    n_chunks = tm // CHUNK
    n_slabs = D_MODEL // SLAB
    heads_per_slab = SLAB // C_HEAD_DIM

    pv = [p for pair in range(n_slabs // 2)
          for p in _proj_pair(xg, win_ref, pair)]
    uz = [_proj_pair(xg, win_ref, n_slabs // 2 + j) for j in range(UZ_EARLY)]
    mu = jnp.sum((pv[0] + pv[1]) + (pv[2] + pv[3]), axis=-1,
                 keepdims=True) * (1.0 / D_MODEL)
    sq = [p * p for p in pv]
    ms = jnp.sum((sq[0] + sq[1]) + (sq[2] + sq[3]), axis=-1,
                 keepdims=True) * (1.0 / D_MODEL)
    var = ms - mu * mu
    vc = [p - mu for p in pv]
    rstd = r * lax.rsqrt(r * r * var + EPS)
    rh = 0.5 * r

    row = lax.broadcasted_iota(jnp.int32, (CHUNK, CHUNK), 0)
    col = lax.broadcasted_iota(jnp.int32, (CHUNK, CHUNK), 1)
    svs = []
    for j in range(n_slabs):
        lanes = slice(j * SLAB, (j + 1) * SLAB)
        vn = (vc[j] * rstd * lng_ref[:, lanes] + lnb_ref[:, lanes]).astype(_BF16)
        sv = []
        for k in range(heads_per_slab):
            hd = j * heads_per_slab + k
            ws = jnp.where(col <= row, ws_ref[hd], 0.0).astype(_BF16)
            hl = slice(k * C_HEAD_DIM, (k + 1) * C_HEAD_DIM)
            rhs = jnp.concatenate(
                [vn[n * CHUNK:(n + 1) * CHUNK, hl] for n in range(n_chunks)],
                axis=-1)
            out = _dot(ws, rhs)
            sv.append(jnp.concatenate(
                [out[:, n * C_HEAD_DIM:(n + 1) * C_HEAD_DIM]
                 for n in range(n_chunks)], axis=0))
        bias = bias_ref[:, lanes]
        svs.append(jnp.concatenate(sv, axis=-1)
                   + jnp.concatenate([bias] * n_chunks, axis=0))
    uz += [_proj_pair(xg, win_ref, n_slabs // 2 + j)
           for j in range(UZ_EARLY, n_slabs)]
    for j in range(n_slabs):
        u, z = uz[j]
        mix_ref[:, j * SLAB:(j + 1) * SLAB] = (
            u * r * svs[j] * _silu_of_double(z * rh)).astype(_BF16)


def _layer_kernel(mix_fn, fold_fn, slab_cols, mix_rows, n_params, *refs):
    x_ref, pre_ref, post_ref, win_ref = refs[:4]
    params = refs[4:4 + n_params]
    (wout_ref, o_ref, mix_scr, win_scr,
     wout_scr) = refs[4 + n_params:9 + n_params]
    carries = refs[9 + n_params:]
    tm = x_ref.shape[1]
    t = pl.program_id(1)

    @pl.when(jnp.logical_and(pl.program_id(0) == 0, t == 0))
    def _():
        for pos, col in enumerate(slab_cols):
            win_scr[:, pos * SLAB:(pos + 1) * SLAB] = fold_fn(
                col, win_ref[:, col:col + SLAB], *params).astype(_BF16)
        for pos, row in enumerate(mix_rows):
            wout_scr[pos * SLAB:(pos + 1) * SLAB, :] = (
                wout_ref[row:row + SLAB, :].astype(_BF16))

    x = x_ref[0]
    xg = (x * pre_ref[...]).astype(_BF16)
    mix_fn(xg, _row_scale(x), mix_scr, t, win_scr, *params, *carries)

    assert sum(POST_ROWS) == tm
    start = 0
    for n_rows in POST_ROWS:
        rows = slice(start, start + n_rows)
        start += n_rows
        m = _dot(mix_scr[rows, :], wout_scr[...])
        o_ref[0, rows, :] = (x_ref[0, rows, :]
                             + m * _row_scale(m) * post_ref[...])


def _full(shape):
    return pl.BlockSpec(shape, lambda b, t: (0,) * len(shape),
                        pipeline_mode=pl.Buffered(1))


def _layer_call(name, mix_fn, fold_fn, slab_cols, mix_rows, x, pre, post, w_in,
                params, w_out, carry_shapes):
    bsz, seq, d = x.shape
    assert sorted(slab_cols) == list(range(0, w_in.shape[1], SLAB))
    assert sorted(mix_rows) == list(range(0, w_out.shape[0], SLAB))
    row_spec = pl.BlockSpec((1, ROW_TILE, d), lambda b, t: (b, t, 0))
    operands = (x, pre.reshape(1, -1), post.reshape(1, -1), w_in, *params,
                w_out)
    scratch = [pltpu.VMEM((ROW_TILE, d), _BF16),
               pltpu.VMEM(w_in.shape, _BF16),
               pltpu.VMEM(w_out.shape, _BF16)]
    scratch += [pltpu.VMEM(s, _F32) for s in carry_shapes]
    return pl.pallas_call(
        functools.partial(_layer_kernel, mix_fn, fold_fn, slab_cols, mix_rows,
                          len(params)),
        name=name,
        grid=(bsz, seq // ROW_TILE),
        in_specs=[row_spec] + [_full(p.shape) for p in operands[1:]],
        out_specs=row_spec,
        out_shape=jax.ShapeDtypeStruct(x.shape, x.dtype),
        scratch_shapes=scratch,
        compiler_params=pltpu.CompilerParams(
            dimension_semantics=("arbitrary", "arbitrary"),
            vmem_limit_bytes=VMEM_LIMIT_BYTES),
    )(*operands)


def _even_fold(col, w, convw_ref, poolw_ref, pscale_ref):
    del convw_ref, pscale_ref
    j, rem = divmod(col - 4 * A_WIDTH, SLAB)
    if 0 <= j < B_WIDTH // SLAB and rem == 0:
        return jnp.dot(w, poolw_ref[j], preferred_element_type=_F32,
                       precision=lax.Precision.HIGHEST)
    return w


def _no_fold(col, w, *params):
    del col, params
    return w


def _even_layer(x, pre, post, w_in, conv_w, pool_w, pool_scale, w_out):
    zero = jnp.zeros_like(pool_w[0])
    pool_bd = jnp.stack([
        jnp.block([[pool_w[2 * j], zero], [zero, pool_w[2 * j + 1]]])
        for j in range(B_WIDTH // SLAB)])
    params = (conv_w, pool_bd, pool_scale.reshape(1, -1))
    carries = ((CONV_HALO, A_WIDTH), (POOL_HALO, B_WIDTH))
    return _layer_call("even_layer", _even_mix, _even_fold, EVEN_SLAB_COLS,
                       EVEN_MIX_ROWS, x, pre, post, w_in, params, w_out,
                       carries)


def _odd_layer(x, pre, post, w_in, ln_g, ln_b, w_s, b_s, w_out):
    bias = jnp.repeat(b_s.T, C_HEAD_DIM, axis=1)
    params = (ln_g.reshape(1, -1), ln_b.reshape(1, -1), w_s, bias)
    return _layer_call("odd_layer", _odd_mix, _no_fold, ODD_SLAB_COLS,
                       ODD_MIX_ROWS, x, pre, post, w_in, params, w_out, ())


def kernel(x, pre_norm, post_norm, even_w_in, even_conv_w, even_pool_w, even_pool_scale, even_w_out, odd_w_in, odd_ln_g, odd_ln_b, odd_w_s, odd_b_s, odd_w_out):
    depth = pre_norm.shape[0]
    for i in range(depth):
        j = i // 2
        if i % 2 == 0:
            x = _even_layer(x, pre_norm[i], post_norm[i], even_w_in[j],
                            even_conv_w[j], even_pool_w[j], even_pool_scale[j],
                            even_w_out[j])
        else:
            x = _odd_layer(x, pre_norm[i], post_norm[i], odd_w_in[j],
                           odd_ln_g[j], odd_ln_b[j], odd_w_s[j], odd_b_s[j],
                           odd_w_out[j])
    return x
```

```python
import functools

import jax
import jax.numpy as jnp
from jax import lax
from jax.experimental import pallas as pl
from jax.experimental.pallas import tpu as pltpu

D_MODEL = 1024
A_WIDTH = 512
B_WIDTH = 512
POOL_WINDOWS = (2, 4, 8, 16)
GROUP_DIM = B_WIDTH // len(POOL_WINDOWS)
C_HEADS = 8
C_HEAD_DIM = 128
CHUNK = 128
EPS = 1e-6

ROW_TILE = 1024
SLAB = 256
CONV_HALO = 8
POOL_HALO = 16
STAGE_ROWS = 256
VMEM_LIMIT_BYTES = 56 * 1024 * 1024

_A, _B = A_WIDTH, B_WIDTH
EVEN_ORDER = (("a", 0), ("a", 1), ("b", 0), ("b", 1))


def _even_cols(kind, j):
    if kind == "a":
        return (0 * _A + j * SLAB, 2 * _A + j * SLAB,
                1 * _A + j * SLAB, 3 * _A + j * SLAB)
    return (4 * _A + j * SLAB, 4 * _A + _B + j * SLAB)


EVEN_SLAB_COLS = tuple(c for kind, j in EVEN_ORDER for c in _even_cols(kind, j))
ODD_SLAB_COLS = tuple(
    D_MODEL + j * SLAB for j in range(D_MODEL // SLAB)
) + tuple(
    col for j in range(D_MODEL // SLAB)
    for col in (j * SLAB, 2 * D_MODEL + j * SLAB))
EVEN_MIX_ROWS = tuple((0 if kind == "a" else _A) + j * SLAB
                      for kind, j in EVEN_ORDER)
ODD_MIX_ROWS = tuple(j * SLAB for j in range(D_MODEL // SLAB))
UZ_EARLY = 1

_F32 = jnp.float32
_BF16 = jnp.bfloat16


def _silu_of_double(half):
    return half + half * jnp.tanh(half)


def _dot(a, b):
    return jnp.dot(a, b, preferred_element_type=_F32)


def _row_scale(v):
    return lax.rsqrt(jnp.mean(v * v, axis=-1, keepdims=True) + EPS)


def _even_mix(proj_pair, r, mix_ref, t, convw_ref, poolw_ref, pscale_ref,
              hc_carry, xp_carry):
    tm = r.shape[0]
    first = t == 0
    groups_per_slab = SLAB // GROUP_DIM
    r2 = r * r
    rh = 0.5 * r
    pos1 = t * tm + lax.broadcasted_iota(jnp.int32, (tm, GROUP_DIM), 0) + 1
    inv_pos = 1.0 / pos1.astype(_F32)

    del poolw_ref

    def slab_b(j, pair, out_lanes):
        lanes = slice(j * SLAB, (j + 1) * SLAB)
        xp, zp = proj_pair(pair)
        xp = xp * r
        carry = jnp.where(first, 0.0, xp_carry[:, lanes])
        xp_carry[:, lanes] = xp[tm - POOL_HALO:, :]
        acc = jnp.concatenate([carry, xp], axis=0)
        shift = 1
        for _ in range(j * groups_per_slab):
            acc = acc + pltpu.roll(acc, shift, axis=0)
            shift *= 2
        pooled = []
        for k in range(groups_per_slab):
            w = POOL_WINDOWS[j * groups_per_slab + k]
            acc = acc + pltpu.roll(acc, shift, axis=0)
            shift *= 2
            inv_count = jnp.where(pos1 < w, inv_pos, 1.0 / w)
            pooled.append(acc[POOL_HALO:, :GROUP_DIM] * inv_count
                          - xp[:, k * GROUP_DIM:(k + 1) * GROUP_DIM])
            acc = acc[:, GROUP_DIM:]
        mixed = jnp.concatenate(pooled, axis=-1)
        mix_ref[:, out_lanes] = (
            mixed * pscale_ref[:, lanes] * _silu_of_double(zp * rh)
        ).astype(_BF16)

    def slab_a(j, pair, out_lanes):
        lanes = slice(j * SLAB, (j + 1) * SLAB)
        xa, gc = proj_pair(pair)
        cw = convw_ref[:, lanes]
        hc = gc * xa * r2
        carry = jnp.where(first, 0.0, hc_carry[:, lanes])
        hc_carry[:, lanes] = hc[tm - CONV_HALO:, :]
        ext = jnp.concatenate([carry, hc], axis=0)
        hc1 = pltpu.roll(ext, 1, axis=0)[CONV_HALO:, :]
        hc2 = pltpu.roll(ext, 2, axis=0)[CONV_HALO:, :]
        conv = cw[0:1] * hc2 + cw[1:2] * hc1 + cw[2:3] * hc
        gb, za = proj_pair(pair + 1)
        mix_ref[:, out_lanes] = (
            gb * r * conv * _silu_of_double(za * rh)).astype(_BF16)

    pair = 0
    for pos, (kind, j) in enumerate(EVEN_ORDER):
        out_lanes = slice(pos * SLAB, (pos + 1) * SLAB)
        if kind == "a":
            slab_a(j, pair, out_lanes)
            pair += 2
        else:
            slab_b(j, pair, out_lanes)
            pair += 1


def _odd_mix(proj_pair, r, mix_ref, t, lng_ref, lnb_ref, ws_ref, bias_ref):
    del t
    tm = r.shape[0]
    n_chunks = tm // CHUNK
    n_slabs = D_MODEL // SLAB
    heads_per_slab = SLAB // C_HEAD_DIM

    pv = [p for pair in range(n_slabs // 2) for p in proj_pair(pair)]
    uz = [proj_pair(n_slabs // 2 + j) for j in range(UZ_EARLY)]
    mu = jnp.sum((pv[0] + pv[1]) + (pv[2] + pv[3]), axis=-1,
                 keepdims=True) * (1.0 / D_MODEL)
    sq = [p * p for p in pv]
    ms = jnp.sum((sq[0] + sq[1]) + (sq[2] + sq[3]), axis=-1,
                 keepdims=True) * (1.0 / D_MODEL)
    var = ms - mu * mu
    vc = [p - mu for p in pv]
    rstd = r * lax.rsqrt(r * r * var + EPS)
    rh = 0.5 * r

    row = lax.broadcasted_iota(jnp.int32, (CHUNK, CHUNK), 0)
    col = lax.broadcasted_iota(jnp.int32, (CHUNK, CHUNK), 1)
    svs = []
    for j in range(n_slabs):
        lanes = slice(j * SLAB, (j + 1) * SLAB)
        vn = (vc[j] * rstd * lng_ref[:, lanes] + lnb_ref[:, lanes]).astype(_BF16)
        sv = []
        for k in range(heads_per_slab):
            hd = j * heads_per_slab + k
            ws = jnp.where(col <= row, ws_ref[hd], 0.0).astype(_BF16)
            hl = slice(k * C_HEAD_DIM, (k + 1) * C_HEAD_DIM)
            rhs = jnp.concatenate(
                [vn[n * CHUNK:(n + 1) * CHUNK, hl] for n in range(n_chunks)],
                axis=-1)
            out = _dot(ws, rhs)
            sv.append(jnp.concatenate(
                [out[:, n * C_HEAD_DIM:(n + 1) * C_HEAD_DIM]
                 for n in range(n_chunks)], axis=0))
        bias = bias_ref[:, lanes]
        svs.append(jnp.concatenate(sv, axis=-1)
                   + jnp.concatenate([bias] * n_chunks, axis=0))
    uz += [proj_pair(n_slabs // 2 + j) for j in range(UZ_EARLY, n_slabs)]
    for j in range(n_slabs):
        u, z = uz[j]
        mix_ref[:, j * SLAB:(j + 1) * SLAB] = (
            u * r * svs[j] * _silu_of_double(z * rh)).astype(_BF16)


def _stage_weights(w_hbm, stage, sem, store_chunk):
    for c in range(w_hbm.shape[0] // STAGE_ROWS):
        copy = pltpu.make_async_copy(
            w_hbm.at[pl.ds(c * STAGE_ROWS, STAGE_ROWS), :], stage, sem)
        copy.start()
        copy.wait()
        store_chunk(c * STAGE_ROWS, stage)


def _layer_kernel(mix_fn, fold_fn, slab_cols, mix_rows, n_params, *refs):
    x_ref, xnext_ref, pre_ref, gcol_ref, post_ref = refs[:5]
    params = refs[5:5 + n_params]
    (win_hbm, wout_hbm, o_ref, mix_scr, win_scr, wout_scr, stage_in, stage_out,
     sem, xg_scr, r_scr, pair0_scr) = refs[5 + n_params:17 + n_params]
    carries = refs[17 + n_params:]
    t = pl.program_id(1)

    def start_tile(tile_ref):
        x = tile_ref[0]
        pair0_scr[...] = lax.dot_general(
            x, win_scr[:, :2 * SLAB], (((1,), (0,)), ((), ())),
            preferred_element_type=_F32)
        xg_scr[...] = (x * pre_ref[...]).astype(_BF16)
        r_scr[...] = _row_scale(x)

    @pl.when(jnp.logical_and(pl.program_id(0) == 0, t == 0))
    def _():
        def store_in(row0, stage):
            for pos, col in enumerate(slab_cols):
                w = fold_fn(col, stage[:, col:col + SLAB], *params)
                if pos < 2:
                    w = w * gcol_ref[row0:row0 + STAGE_ROWS, :]
                win_scr[row0:row0 + STAGE_ROWS,
                        pos * SLAB:(pos + 1) * SLAB] = w.astype(_BF16)

        def store_out(row0, stage):
            dst = mix_rows.index(row0) * SLAB
            wout_scr[dst:dst + STAGE_ROWS, :] = stage[...].astype(_BF16)

        _stage_weights(win_hbm, stage_in, sem, store_in)
        _stage_weights(wout_hbm, stage_out, sem, store_out)
        start_tile(x_ref)

    def proj_pair(pair):
        cols = slice(2 * pair * SLAB, (2 * pair + 2) * SLAB)
        out = pair0_scr[...] if pair == 0 else _dot(xg_scr[...], win_scr[:, cols])
        return out[:, :SLAB], out[:, SLAB:]

    mix_fn(proj_pair, r_scr[...], mix_scr, t, *params, *carries)

    m = _dot(mix_scr[...], wout_scr[...])
    o_ref[0] = x_ref[0] + m * _row_scale(m) * post_ref[...]

    start_tile(xnext_ref)


def _full(shape):
    return pl.BlockSpec(shape, lambda b, t: (0,) * len(shape),
                        pipeline_mode=pl.Buffered(1))


def _layer_call(name, mix_fn, fold_fn, slab_cols, mix_rows, x, pre, post, w_in,
                params, w_out, carry_shapes):
    bsz, seq, d = x.shape
    tiles = seq // ROW_TILE
    assert STAGE_ROWS == SLAB
    assert sorted(slab_cols) == list(range(0, w_in.shape[1], SLAB))
    assert sorted(mix_rows) == list(range(0, w_out.shape[0], SLAB))

    def next_tile(b, t):
        flat = jnp.minimum(b * tiles + t + 1, bsz * tiles - 1)
        return (flat // tiles, flat % tiles, 0)

    row_block = (1, ROW_TILE, d)
    small = (pre.reshape(1, -1), pre.reshape(-1, 1), post.reshape(1, -1),
             *params)
    scratch = [pltpu.VMEM((ROW_TILE, d), _BF16),
               pltpu.VMEM(w_in.shape, _BF16),
               pltpu.VMEM(w_out.shape, _BF16),
               pltpu.VMEM((STAGE_ROWS, w_in.shape[1]), _F32),
               pltpu.VMEM((STAGE_ROWS, w_out.shape[1]), _F32),
               pltpu.SemaphoreType.DMA,
               pltpu.VMEM((ROW_TILE, d), _BF16),
               pltpu.VMEM((ROW_TILE, 1), _F32),
               pltpu.VMEM((ROW_TILE, 2 * SLAB), _F32)]
    scratch += [pltpu.VMEM(s, _F32) for s in carry_shapes]
    return pl.pallas_call(
        functools.partial(_layer_kernel, mix_fn, fold_fn, slab_cols, mix_rows,
                          len(params)),
        name=name,
        grid=(bsz, tiles),
        in_specs=([pl.BlockSpec(row_block, lambda b, t: (b, t, 0)),
                   pl.BlockSpec(row_block, next_tile)]
                  + [_full(p.shape) for p in small]
                  + [pl.BlockSpec(memory_space=pl.ANY)] * 2),
        out_specs=pl.BlockSpec(row_block, lambda b, t: (b, t, 0)),
        out_shape=jax.ShapeDtypeStruct(x.shape, x.dtype),
        scratch_shapes=scratch,
        compiler_params=pltpu.CompilerParams(
            dimension_semantics=("arbitrary", "arbitrary"),
            vmem_limit_bytes=VMEM_LIMIT_BYTES),
    )(x, x, *small, w_in, w_out)


def _even_fold(col, w, convw_ref, poolw_ref, pscale_ref):
    del convw_ref, pscale_ref
    j, rem = divmod(col - 4 * A_WIDTH, SLAB)
    if 0 <= j < B_WIDTH // SLAB and rem == 0:
        return jnp.dot(w, poolw_ref[j], preferred_element_type=_F32,
                       precision=lax.Precision.HIGHEST)
    return w


def _no_fold(col, w, *params):
    del col, params
    return w


def _even_layer(x, pre, post, w_in, conv_w, pool_w, pool_scale, w_out):
    zero = jnp.zeros_like(pool_w[0])
    pool_bd = jnp.stack([
        jnp.block([[pool_w[2 * j], zero], [zero, pool_w[2 * j + 1]]])
        for j in range(B_WIDTH // SLAB)])
    params = (conv_w, pool_bd, pool_scale.reshape(1, -1))
    carries = ((CONV_HALO, A_WIDTH), (POOL_HALO, B_WIDTH))
    return _layer_call("even_layer", _even_mix, _even_fold, EVEN_SLAB_COLS,
                       EVEN_MIX_ROWS, x, pre, post, w_in, params, w_out,
                       carries)


def _odd_layer(x, pre, post, w_in, ln_g, ln_b, w_s, b_s, w_out):
    bias = jnp.repeat(b_s.T, C_HEAD_DIM, axis=1)
    params = (ln_g.reshape(1, -1), ln_b.reshape(1, -1), w_s, bias)
    return _layer_call("odd_layer", _odd_mix, _no_fold, ODD_SLAB_COLS,
                       ODD_MIX_ROWS, x, pre, post, w_in, params, w_out, ())


def kernel(x, pre_norm, post_norm, even_w_in, even_conv_w, even_pool_w, even_pool_scale, even_w_out, odd_w_in, odd_ln_g, odd_ln_b, odd_w_s, odd_b_s, odd_w_out):
    depth = pre_norm.shape[0]
    for i in range(depth):
        j = i // 2
        if i % 2 == 0:
            x = _even_layer(x, pre_norm[i], post_norm[i], even_w_in[j],
                            even_conv_w[j], even_pool_w[j], even_pool_scale[j],
                            even_w_out[j])
        else:
            x = _odd_layer(x, pre_norm[i], post_norm[i], odd_w_in[j],
                           odd_ln_g[j], odd_ln_b[j], odd_w_s[j], odd_b_s[j],
                           odd_w_out[j])
    return x
```

```python
import functools

import jax
import jax.numpy as jnp
from jax import lax
from jax.experimental import pallas as pl
from jax.experimental.pallas import tpu as pltpu

D_MODEL = 1024
A_WIDTH = 512
B_WIDTH = 512
POOL_WINDOWS = (2, 4, 8, 16)
GROUP_DIM = B_WIDTH // len(POOL_WINDOWS)
C_HEADS = 8
C_HEAD_DIM = 128
CHUNK = 128
EPS = 1e-6

ROW_TILE = 1024
POST_ROWS = (1024,)
SLAB = 256
CONV_HALO = 8
POOL_HALO = 16
VMEM_LIMIT_BYTES = 56 * 1024 * 1024

_A, _B = A_WIDTH, B_WIDTH
EVEN_ORDER = (("a", 0), ("a", 1), ("b", 0), ("b", 1))


def _even_cols(kind, j):
    if kind == "a":
        return (0 * _A + j * SLAB, 2 * _A + j * SLAB,
                1 * _A + j * SLAB, 3 * _A + j * SLAB)
    return (4 * _A + j * SLAB, 4 * _A + _B + j * SLAB)


EVEN_SLAB_COLS = tuple(c for kind, j in EVEN_ORDER for c in _even_cols(kind, j))
ODD_SLAB_COLS = tuple(
    D_MODEL + j * SLAB for j in range(D_MODEL // SLAB)
) + tuple(
    col for j in range(D_MODEL // SLAB)
    for col in (j * SLAB, 2 * D_MODEL + j * SLAB))
EVEN_MIX_ROWS = tuple((0 if kind == "a" else _A) + j * SLAB
                      for kind, j in EVEN_ORDER)
ODD_MIX_ROWS = tuple(j * SLAB for j in range(D_MODEL // SLAB))
UZ_EARLY = 1

_F32 = jnp.float32
_BF16 = jnp.bfloat16


def _silu_of_double(half):
    return half + half * jnp.tanh(half)


def _dot(a, b):
    return jnp.dot(a, b, preferred_element_type=_F32)


def _row_scale(v):
    return lax.rsqrt(jnp.mean(v * v, axis=-1, keepdims=True) + EPS)


def _proj_pair(xg, win_ref, pair):
    out = _dot(xg, win_ref[:, 2 * pair * SLAB:(2 * pair + 2) * SLAB])
    return out[:, :SLAB], out[:, SLAB:]


def _even_mix(xg, r, mix_ref, t, win_ref, convw_ref, poolw_ref, pscale_ref,
              hc_carry, xp_carry):
    tm = xg.shape[0]
    first = t == 0
    groups_per_slab = SLAB // GROUP_DIM
    r2 = r * r
    rh = 0.5 * r
    pos1 = t * tm + lax.broadcasted_iota(jnp.int32, (tm, GROUP_DIM), 0) + 1
    inv_pos = 1.0 / pos1.astype(_F32)

    del poolw_ref

    def slab_b(j, pair, out_lanes):
        lanes = slice(j * SLAB, (j + 1) * SLAB)
        xp, zp = _proj_pair(xg, win_ref, pair)
        xp = xp * r
        carry = jnp.where(first, 0.0, xp_carry[:, lanes])
        xp_carry[:, lanes] = xp[tm - POOL_HALO:, :]
        acc = jnp.concatenate([carry, xp], axis=0)
        shift = 1
        for _ in range(j * groups_per_slab):
            acc = acc + pltpu.roll(acc, shift, axis=0)
            shift *= 2
        pooled = []
        for k in range(groups_per_slab):
            w = POOL_WINDOWS[j * groups_per_slab + k]
            acc = acc + pltpu.roll(acc, shift, axis=0)
            shift *= 2
            inv_count = jnp.where(pos1 < w, inv_pos, 1.0 / w)
            pooled.append(acc[POOL_HALO:, :GROUP_DIM] * inv_count
                          - xp[:, k * GROUP_DIM:(k + 1) * GROUP_DIM])
            acc = acc[:, GROUP_DIM:]
        mixed = jnp.concatenate(pooled, axis=-1)
        mix_ref[:, out_lanes] = (
            mixed * pscale_ref[:, lanes] * _silu_of_double(zp * rh)
        ).astype(_BF16)

    def slab_a(j, pair, out_lanes):
        lanes = slice(j * SLAB, (j + 1) * SLAB)
        xa, gc = _proj_pair(xg, win_ref, pair)
        cw = convw_ref[:, lanes]
        hc = gc * xa * r2
        carry = jnp.where(first, 0.0, hc_carry[:, lanes])
        hc_carry[:, lanes] = hc[tm - CONV_HALO:, :]
        ext = jnp.concatenate([carry, hc], axis=0)
        hc1 = pltpu.roll(ext, 1, axis=0)[CONV_HALO:, :]
        hc2 = pltpu.roll(ext, 2, axis=0)[CONV_HALO:, :]
        conv = cw[0:1] * hc2 + cw[1:2] * hc1 + cw[2:3] * hc
        gb, za = _proj_pair(xg, win_ref, pair + 1)
        mix_ref[:, out_lanes] = (
            gb * r * conv * _silu_of_double(za * rh)).astype(_BF16)

    pair = 0
    for pos, (kind, j) in enumerate(EVEN_ORDER):
        out_lanes = slice(pos * SLAB, (pos + 1) * SLAB)
        if kind == "a":
            slab_a(j, pair, out_lanes)
            pair += 2
        else:
            slab_b(j, pair, out_lanes)
            pair += 1


def _odd_mix(xg, r, mix_ref, t, win_ref, lng_ref, lnb_ref, ws_ref, bias_ref):
    del t
    tm = xg.shape[0]
    n_chunks = tm // CHUNK
    n_slabs = D_MODEL // SLAB
    heads_per_slab = SLAB // C_HEAD_DIM

    pv = [p for pair in range(n_slabs // 2)
          for p in _proj_pair(xg, win_ref, pair)]
    uz = [_proj_pair(xg, win_ref, n_slabs // 2 + j) for j in range(UZ_EARLY)]
    mu = jnp.sum((pv[0] + pv[1]) + (pv[2] + pv[3]), axis=-1,
                 keepdims=True) * (1.0 / D_MODEL)
    sq = [p * p for p in pv]
    ms = jnp.sum((sq[0] + sq[1]) + (sq[2] + sq[3]), axis=-1,
                 keepdims=True) * (1.0 / D_MODEL)
    var = ms - mu * mu
    vc = [p - mu for p in pv]
    rstd = r * lax.rsqrt(r * r * var + EPS)
    rh = 0.5 * r

    row = lax.broadcasted_iota(jnp.int32, (CHUNK, CHUNK), 0)
    col = lax.broadcasted_iota(jnp.int32, (CHUNK, CHUNK), 1)
    svs = []
    for j in range(n_slabs):
        lanes = slice(j * SLAB, (j + 1) * SLAB)
        vn = (vc[j] * rstd * lng_ref[:, lanes] + lnb_ref[:, lanes]).astype(_BF16)
        sv = []
        for k in range(heads_per_slab):
            hd = j * heads_per_slab + k
            ws = jnp.where(col <= row, ws_ref[hd], 0.0).astype(_BF16)
            hl = slice(k * C_HEAD_DIM, (k + 1) * C_HEAD_DIM)
            rhs = jnp.concatenate(
                [vn[n * CHUNK:(n + 1) * CHUNK, hl] for n in range(n_chunks)],
                axis=-1)
            out = _dot(ws, rhs)
            sv.append(jnp.concatenate(
                [out[:, n * C_HEAD_DIM:(n + 1) * C_HEAD_DIM]
                 for n in range(n_chunks)], axis=0))
        bias = bias_ref[:, lanes]
        svs.append(jnp.concatenate(sv, axis=-1)
                   + jnp.concatenate([bias] * n_chunks, axis=0))
    uz += [_proj_pair(xg, win_ref, n_slabs // 2 + j)
           for j in range(UZ_EARLY, n_slabs)]
    for j in range(n_slabs):
        u, z = uz[j]
        mix_ref[:, j * SLAB:(j + 1) * SLAB] = (
            u * r * svs[j] * _silu_of_double(z * rh)).astype(_BF16)


def _layer_kernel(layer, idx, mix_fn, fold_fn, view_fn, slab_cols, mix_rows,
                  n_params, n_extra, *refs):
    x_ref, pre_ref, post_ref, win_ref = refs[:4]
    raw_params = refs[4:4 + n_params]
    (wout_ref, o_ref, mix_scr, win_scr,
     wout_scr) = refs[4 + n_params:9 + n_params]
    extra = refs[9 + n_params:9 + n_params + n_extra]
    carries = refs[9 + n_params + n_extra:]
    tm = x_ref.shape[1]
    t = pl.program_id(1)
    params = view_fn(idx, raw_params, extra)

    @pl.when(jnp.logical_and(pl.program_id(0) == 0, t == 0))
    def _():
        view_fn(idx, raw_params, extra, init=True)
        for pos, col in enumerate(slab_cols):
            win_scr[:, pos * SLAB:(pos + 1) * SLAB] = fold_fn(
                col, win_ref[idx, :, col:col + SLAB], *params).astype(_BF16)
        for pos, row in enumerate(mix_rows):
            wout_scr[pos * SLAB:(pos + 1) * SLAB, :] = (
                wout_ref[idx, row:row + SLAB, :].astype(_BF16))

    x = x_ref[0]
    xg = (x * pre_ref[layer]).astype(_BF16)
    mix_fn(xg, _row_scale(x), mix_scr, t, win_scr, *params, *carries)

    assert sum(POST_ROWS) == tm
    start = 0
    for n_rows in POST_ROWS:
        rows = slice(start, start + n_rows)
        start += n_rows
        m = _dot(mix_scr[rows, :], wout_scr[...])
        o_ref[0, rows, :] = (x_ref[0, rows, :]
                             + m * _row_scale(m) * post_ref[layer])


def _full(shape):
    return pl.BlockSpec(shape, lambda b, t: (0,) * len(shape),
                        pipeline_mode=pl.Buffered(1))


def _layer_call(name, layer, idx, mix_fn, fold_fn, view_fn, slab_cols, mix_rows,
                x, pre, post, w_in, params, w_out, extra_shapes, carry_shapes):
    bsz, seq, d = x.shape
    assert sorted(slab_cols) == list(range(0, w_in.shape[2], SLAB))
    assert sorted(mix_rows) == list(range(0, w_out.shape[1], SLAB))
    row_spec = pl.BlockSpec((1, ROW_TILE, d), lambda b, t: (b, t, 0))
    operands = (x, pre.reshape(-1, 1, d), post.reshape(-1, 1, d), w_in, *params,
                w_out)
    scratch = [pltpu.VMEM((ROW_TILE, d), _BF16),
               pltpu.VMEM(w_in.shape[1:], _BF16),
               pltpu.VMEM(w_out.shape[1:], _BF16)]
    scratch += [pltpu.VMEM(s, _F32) for s in extra_shapes + carry_shapes]
    return pl.pallas_call(
        functools.partial(_layer_kernel, layer, idx, mix_fn, fold_fn, view_fn,
                          slab_cols, mix_rows, len(params), len(extra_shapes)),
        name=name,
        grid=(bsz, seq // ROW_TILE),
        in_specs=[row_spec] + [_full(p.shape) for p in operands[1:]],
        out_specs=row_spec,
        out_shape=jax.ShapeDtypeStruct(x.shape, x.dtype),
        scratch_shapes=scratch,
        compiler_params=pltpu.CompilerParams(
            dimension_semantics=("arbitrary", "arbitrary"),
            vmem_limit_bytes=VMEM_LIMIT_BYTES),
    )(*operands)


def _even_fold(col, w, convw_ref, poolw_ref, pscale_ref):
    del convw_ref, pscale_ref
    j, rem = divmod(col - 4 * A_WIDTH, SLAB)
    if 0 <= j < B_WIDTH // SLAB and rem == 0:
        groups_per_slab = SLAB // GROUP_DIM
        return jnp.concatenate([
            jnp.dot(w[:, k * GROUP_DIM:(k + 1) * GROUP_DIM],
                    poolw_ref[j * groups_per_slab + k],
                    preferred_element_type=_F32,
                    precision=lax.Precision.HIGHEST)
            for k in range(groups_per_slab)], axis=-1)
    return w


def _no_fold(col, w, *params):
    del col, params
    return w


def _even_views(idx, raw, extra, init=False):
    del extra, init
    convw_ref, poolw_ref, pscale_ref = raw
    return (convw_ref.at[idx], poolw_ref.at[idx], pscale_ref.at[idx:idx + 1])


def _odd_views(idx, raw, extra, init=False):
    lng_ref, lnb_ref, ws_ref, bs_ref = raw
    (bias_scr,) = extra
    if init:
        n_cols = C_HEADS * C_HEAD_DIM
        head_of_col = lax.broadcasted_iota(
            jnp.int32, (C_HEADS, n_cols), 1) // C_HEAD_DIM
        select = (head_of_col == lax.broadcasted_iota(
            jnp.int32, (C_HEADS, n_cols), 0)).astype(_F32)
        bias_scr[...] = lax.dot_general(
            bs_ref[idx], select, (((0,), (0,)), ((), ())),
            preferred_element_type=_F32, precision=lax.Precision.HIGHEST)
    return (lng_ref.at[idx:idx + 1], lnb_ref.at[idx:idx + 1], ws_ref.at[idx],
            bias_scr)


def kernel(x, pre_norm, post_norm, even_w_in, even_conv_w, even_pool_w, even_pool_scale, even_w_out, odd_w_in, odd_ln_g, odd_ln_b, odd_w_s, odd_b_s, odd_w_out):
    depth = pre_norm.shape[0]
    for i in range(depth):
        j = i // 2
        if i % 2 == 0:
            params = (even_conv_w, even_pool_w, even_pool_scale)
            carries = [(CONV_HALO, A_WIDTH), (POOL_HALO, B_WIDTH)]
            x = _layer_call("even_layer", i, j, _even_mix, _even_fold,
                            _even_views, EVEN_SLAB_COLS, EVEN_MIX_ROWS, x,
                            pre_norm, post_norm, even_w_in, params, even_w_out,
                            [], carries)
        else:
            params = (odd_ln_g, odd_ln_b, odd_w_s, odd_b_s)
            x = _layer_call("odd_layer", i, j, _odd_mix, _no_fold, _odd_views,
                            ODD_SLAB_COLS, ODD_MIX_ROWS, x, pre_norm, post_norm,
                            odd_w_in, params, odd_w_out,
                            [(CHUNK, C_HEADS * C_HEAD_DIM)], [])
    return x
```

```python
import functools

import jax
import jax.numpy as jnp
from jax import lax
from jax.experimental import pallas as pl
from jax.experimental.pallas import tpu as pltpu

D_MODEL = 1024
A_WIDTH = 512
B_WIDTH = 512
POOL_WINDOWS = (2, 4, 8, 16)
GROUP_DIM = B_WIDTH // len(POOL_WINDOWS)
C_HEADS = 8
C_HEAD_DIM = 128
CHUNK = 128
EPS = 1e-6

ROW_TILE = 1024
POST_ROWS = (1024,)
SLAB = 256
CONV_HALO = 8
POOL_HALO = 16
VMEM_LIMIT_BYTES = 56 * 1024 * 1024

_A, _B = A_WIDTH, B_WIDTH
EVEN_ORDER = (("a", 0), ("a", 1), ("b", 0), ("b", 1))


def _even_cols(kind, j):
    if kind == "a":
        return (0 * _A + j * SLAB, 2 * _A + j * SLAB,
                1 * _A + j * SLAB, 3 * _A + j * SLAB)
    return (4 * _A + j * SLAB, 4 * _A + _B + j * SLAB)


EVEN_SLAB_COLS = tuple(c for kind, j in EVEN_ORDER for c in _even_cols(kind, j))
ODD_SLAB_COLS = tuple(
    D_MODEL + j * SLAB for j in range(D_MODEL // SLAB)
) + tuple(
    col for j in range(D_MODEL // SLAB)
    for col in (j * SLAB, 2 * D_MODEL + j * SLAB))
EVEN_MIX_ROWS = tuple((0 if kind == "a" else _A) + j * SLAB
                      for kind, j in EVEN_ORDER)
ODD_MIX_ROWS = tuple(j * SLAB for j in range(D_MODEL // SLAB))
UZ_EARLY = 1

_F32 = jnp.float32
_BF16 = jnp.bfloat16


def _silu_of_double(half):
    return half + half * jnp.tanh(half)


def _dot(a, b):
    return jnp.dot(a, b, preferred_element_type=_F32)


def _row_scale(v):
    return lax.rsqrt(jnp.mean(v * v, axis=-1, keepdims=True) + EPS)


def _proj_pair(xg, win_ref, pair):
    out = _dot(xg, win_ref[:, 2 * pair * SLAB:(2 * pair + 2) * SLAB])
    return out[:, :SLAB], out[:, SLAB:]


def _even_mix(xg, r, mix_ref, t, win_ref, convw_ref, poolw_ref, pscale_ref,
              hc_carry, xp_carry):
    tm = xg.shape[0]
    first = t == 0
    groups_per_slab = SLAB // GROUP_DIM
    r2 = r * r
    rh = 0.5 * r
    pos1 = t * tm + lax.broadcasted_iota(jnp.int32, (tm, GROUP_DIM), 0) + 1
    inv_pos = 1.0 / pos1.astype(_F32)

    del poolw_ref

    def slab_b(j, pair, out_lanes):
        lanes = slice(j * SLAB, (j + 1) * SLAB)
        xp, zp = _proj_pair(xg, win_ref, pair)
        xp = xp * r
        carry = jnp.where(first, 0.0, xp_carry[:, lanes])
        xp_carry[:, lanes] = xp[tm - POOL_HALO:, :]
        acc = jnp.concatenate([carry, xp], axis=0)
        shift = 1
        for _ in range(j * groups_per_slab):
            acc = acc + pltpu.roll(acc, shift, axis=0)
            shift *= 2
        pooled = []
        for k in range(groups_per_slab):
            w = POOL_WINDOWS[j * groups_per_slab + k]
            acc = acc + pltpu.roll(acc, shift, axis=0)
            shift *= 2
            inv_count = jnp.where(pos1 < w, inv_pos, 1.0 / w)
            pooled.append(acc[POOL_HALO:, :GROUP_DIM] * inv_count
                          - xp[:, k * GROUP_DIM:(k + 1) * GROUP_DIM])
            acc = acc[:, GROUP_DIM:]
        mixed = jnp.concatenate(pooled, axis=-1)
        mix_ref[:, out_lanes] = (
            mixed * pscale_ref[:, lanes] * _silu_of_double(zp * rh)
        ).astype(_BF16)

    def slab_a(j, pair, out_lanes):
        lanes = slice(j * SLAB, (j + 1) * SLAB)
        xa, gc = _proj_pair(xg, win_ref, pair)
        cw = convw_ref[:, lanes]
        hc = gc * xa * r2
        carry = jnp.where(first, 0.0, hc_carry[:, lanes])
        hc_carry[:, lanes] = hc[tm - CONV_HALO:, :]
        ext = jnp.concatenate([carry, hc], axis=0)
        hc1 = pltpu.roll(ext, 1, axis=0)[CONV_HALO:, :]
        hc2 = pltpu.roll(ext, 2, axis=0)[CONV_HALO:, :]
        conv = cw[0:1] * hc2 + cw[1:2] * hc1 + cw[2:3] * hc
        gb, za = _proj_pair(xg, win_ref, pair + 1)
        mix_ref[:, out_lanes] = (
            gb * r * conv * _silu_of_double(za * rh)).astype(_BF16)

    pair = 0
    for pos, (kind, j) in enumerate(EVEN_ORDER):
        out_lanes = slice(pos * SLAB, (pos + 1) * SLAB)
        if kind == "a":
            slab_a(j, pair, out_lanes)
            pair += 2
        else:
            slab_b(j, pair, out_lanes)
            pair += 1


def _odd_mix(xg, r, mix_ref, t, win_ref, lng_ref, lnb_ref, ws_ref, bias_ref):
    del t
    tm = xg.shape[0]
    n_chunks = tm // CHUNK
    n_slabs = D_MODEL // SLAB
    heads_per_slab = SLAB // C_HEAD_DIM

    pv = [p for pair in range(n_slabs // 2)
          for p in _proj_pair(xg, win_ref, pair)]
    uz = [_proj_pair(xg, win_ref, n_slabs // 2 + j) for j in range(UZ_EARLY)]
    mu = jnp.sum((pv[0] + pv[1]) + (pv[2] + pv[3]), axis=-1,
                 keepdims=True) * (1.0 / D_MODEL)
    sq = [p * p for p in pv]
    ms = jnp.sum((sq[0] + sq[1]) + (sq[2] + sq[3]), axis=-1,
                 keepdims=True) * (1.0 / D_MODEL)
    var = ms - mu * mu
    vc = [p - mu for p in pv]
    rstd = r * lax.rsqrt(r * r * var + EPS)
    rh = 0.5 * r

    row = lax.broadcasted_iota(jnp.int32, (CHUNK, CHUNK), 0)
    col = lax.broadcasted_iota(jnp.int32, (CHUNK, CHUNK), 1)
    svs = []
    for j in range(n_slabs):
        lanes = slice(j * SLAB, (j + 1) * SLAB)
        vn = (vc[j] * rstd * lng_ref[:, lanes] + lnb_ref[:, lanes]).astype(_BF16)
        sv = []
        for k in range(heads_per_slab):
            hd = j * heads_per_slab + k
            ws = jnp.where(col <= row, ws_ref[hd], 0.0).astype(_BF16)
            hl = slice(k * C_HEAD_DIM, (k + 1) * C_HEAD_DIM)
            rhs = jnp.concatenate(
                [vn[n * CHUNK:(n + 1) * CHUNK, hl] for n in range(n_chunks)],
                axis=-1)
            out = _dot(ws, rhs)
            sv.append(jnp.concatenate(
                [out[:, n * C_HEAD_DIM:(n + 1) * C_HEAD_DIM]
                 for n in range(n_chunks)], axis=0))
        bias = bias_ref[:, lanes]
        svs.append(jnp.concatenate(sv, axis=-1)
                   + jnp.concatenate([bias] * n_chunks, axis=0))
    uz += [_proj_pair(xg, win_ref, n_slabs // 2 + j)
           for j in range(UZ_EARLY, n_slabs)]
    for j in range(n_slabs):
        u, z = uz[j]
        mix_ref[:, j * SLAB:(j + 1) * SLAB] = (
            u * r * svs[j] * _silu_of_double(z * rh)).astype(_BF16)


def _layer_kernel(layer, idx, mix_fn, fold_fn, view_fn, slab_cols, mix_rows,
                  n_params, n_extra, *refs):
    x_ref, pre_ref, post_ref, win_ref = refs[:4]
    raw_params = refs[4:4 + n_params]
    (wout_ref, o_ref, mix_scr, win_scr,
     wout_scr) = refs[4 + n_params:9 + n_params]
    extra = refs[9 + n_params:9 + n_params + n_extra]
    carries = refs[9 + n_params + n_extra:]
    tm = x_ref.shape[1]
    t = pl.program_id(1)
    params = view_fn(idx, raw_params, extra)

    @pl.when(jnp.logical_and(pl.program_id(0) == 0, t == 0))
    def _():
        view_fn(idx, raw_params, extra, init=True)
        for pos, col in enumerate(slab_cols):
            win_scr[:, pos * SLAB:(pos + 1) * SLAB] = fold_fn(
                col, win_ref[idx, :, col:col + SLAB], *params).astype(_BF16)
        for pos, row in enumerate(mix_rows):
            wout_scr[pos * SLAB:(pos + 1) * SLAB, :] = (
                wout_ref[idx, row:row + SLAB, :].astype(_BF16))

    x = x_ref[0]
    xg = (x * pre_ref[layer:layer + 1, :]).astype(_BF16)
    mix_fn(xg, _row_scale(x), mix_scr, t, win_scr, *params, *carries)

    assert sum(POST_ROWS) == tm
    start = 0
    for n_rows in POST_ROWS:
        rows = slice(start, start + n_rows)
        start += n_rows
        m = _dot(mix_scr[rows, :], wout_scr[...])
        o_ref[0, rows, :] = (x_ref[0, rows, :]
                             + m * _row_scale(m) * post_ref[layer:layer + 1, :])


def _full(shape):
    return pl.BlockSpec(shape, lambda b, t: (0,) * len(shape),
                        pipeline_mode=pl.Buffered(1))


def _layer_call(name, layer, idx, mix_fn, fold_fn, view_fn, slab_cols, mix_rows,
                x, pre, post, w_in, params, w_out, extra_shapes, carry_shapes):
    bsz, seq, d = x.shape
    assert sorted(slab_cols) == list(range(0, w_in.shape[2], SLAB))
    assert sorted(mix_rows) == list(range(0, w_out.shape[1], SLAB))
    row_spec = pl.BlockSpec((1, ROW_TILE, d), lambda b, t: (b, t, 0))
    operands = (x, pre, post, w_in, *params, w_out)
    scratch = [pltpu.VMEM((ROW_TILE, d), _BF16),
               pltpu.VMEM(w_in.shape[1:], _BF16),
               pltpu.VMEM(w_out.shape[1:], _BF16)]
    scratch += [pltpu.VMEM(s, _F32) for s in extra_shapes + carry_shapes]
    return pl.pallas_call(
        functools.partial(_layer_kernel, layer, idx, mix_fn, fold_fn, view_fn,
                          slab_cols, mix_rows, len(params), len(extra_shapes)),
        name=name,
        grid=(bsz, seq // ROW_TILE),
        in_specs=[row_spec] + [_full(p.shape) for p in operands[1:]],
        out_specs=row_spec,
        out_shape=jax.ShapeDtypeStruct(x.shape, x.dtype),
        scratch_shapes=scratch,
        compiler_params=pltpu.CompilerParams(
            dimension_semantics=("arbitrary", "arbitrary"),
            vmem_limit_bytes=VMEM_LIMIT_BYTES),
    )(*operands)


def _even_fold(col, w, convw_ref, poolw_ref, pscale_ref):
    del convw_ref, pscale_ref
    j, rem = divmod(col - 4 * A_WIDTH, SLAB)
    if 0 <= j < B_WIDTH // SLAB and rem == 0:
        groups_per_slab = SLAB // GROUP_DIM
        return jnp.concatenate([
            jnp.dot(w[:, k * GROUP_DIM:(k + 1) * GROUP_DIM],
                    poolw_ref[j * groups_per_slab + k],
                    preferred_element_type=_F32,
                    precision=lax.Precision.HIGHEST)
            for k in range(groups_per_slab)], axis=-1)
    return w


def _no_fold(col, w, *params):
    del col, params
    return w


def _even_views(idx, raw, extra, init=False):
    del extra, init
    convw_ref, poolw_ref, pscale_ref = raw
    return (convw_ref.at[idx], poolw_ref.at[idx], pscale_ref.at[idx:idx + 1])


def _odd_views(idx, raw, extra, init=False):
    lng_ref, lnb_ref, ws_ref, bs_ref = raw
    (bias_scr,) = extra
    if init:
        n_cols = C_HEADS * C_HEAD_DIM
        head_of_col = lax.broadcasted_iota(
            jnp.int32, (C_HEADS, n_cols), 1) // C_HEAD_DIM
        select = (head_of_col == lax.broadcasted_iota(
            jnp.int32, (C_HEADS, n_cols), 0)).astype(_F32)
        bias_scr[...] = lax.dot_general(
            bs_ref[idx], select, (((0,), (0,)), ((), ())),
            preferred_element_type=_F32, precision=lax.Precision.HIGHEST)
    return (lng_ref.at[idx:idx + 1], lnb_ref.at[idx:idx + 1], ws_ref.at[idx],
            bias_scr)


def kernel(x, pre_norm, post_norm, even_w_in, even_conv_w, even_pool_w, even_pool_scale, even_w_out, odd_w_in, odd_ln_g, odd_ln_b, odd_w_s, odd_b_s, odd_w_out):
    depth = pre_norm.shape[0]
    for i in range(depth):
        j = i // 2
        if i % 2 == 0:
            params = (even_conv_w, even_pool_w, even_pool_scale)
            carries = [(CONV_HALO, A_WIDTH), (POOL_HALO, B_WIDTH)]
            x = _layer_call("even_layer", i, j, _even_mix, _even_fold,
                            _even_views, EVEN_SLAB_COLS, EVEN_MIX_ROWS, x,
                            pre_norm, post_norm, even_w_in, params, even_w_out,
                            [], carries)
        else:
            params = (odd_ln_g, odd_ln_b, odd_w_s, odd_b_s)
            x = _layer_call("odd_layer", i, j, _odd_mix, _no_fold, _odd_views,
                            ODD_SLAB_COLS, ODD_MIX_ROWS, x, pre_norm, post_norm,
                            odd_w_in, params, odd_w_out,
                            [(CHUNK, C_HEADS * C_HEAD_DIM)], [])
    return x
```

```python
import functools

import jax
import jax.numpy as jnp
from jax import lax
from jax.experimental import pallas as pl
from jax.experimental.pallas import tpu as pltpu

D_MODEL = 1024
A_WIDTH = 512
B_WIDTH = 512
POOL_WINDOWS = (2, 4, 8, 16)
GROUP_DIM = B_WIDTH // len(POOL_WINDOWS)
C_HEADS = 8
C_HEAD_DIM = 128
CHUNK = 128
EPS = 1e-6

ROW_TILE = 1024
SLAB = 256
CONV_HALO = 8
POOL_HALO = 16
STAGE_ROWS = 256
VMEM_LIMIT_BYTES = 56 * 1024 * 1024

_A, _B = A_WIDTH, B_WIDTH
EVEN_ORDER = (("a", 0), ("b", 0), ("b", 1), ("a", 1))


def _even_cols(kind, j):
    if kind == "a":
        return (0 * _A + j * SLAB, 2 * _A + j * SLAB,
                1 * _A + j * SLAB, 3 * _A + j * SLAB)
    return (4 * _A + j * SLAB, 4 * _A + _B + j * SLAB)


EVEN_SLAB_COLS = tuple(c for kind, j in EVEN_ORDER for c in _even_cols(kind, j))
ODD_SLAB_COLS = tuple(
    D_MODEL + j * SLAB for j in range(D_MODEL // SLAB)
) + tuple(
    col for j in range(D_MODEL // SLAB)
    for col in (j * SLAB, 2 * D_MODEL + j * SLAB))
EVEN_MIX_ROWS = tuple((0 if kind == "a" else _A) + j * SLAB
                      for kind, j in EVEN_ORDER)
ODD_MIX_ROWS = tuple(j * SLAB for j in range(D_MODEL // SLAB))
UZ_EARLY = 1

_F32 = jnp.float32
_BF16 = jnp.bfloat16


def _silu_of_double(half):
    return half + half * jnp.tanh(half)


def _dot(a, b):
    return jnp.dot(a, b, preferred_element_type=_F32)


def _row_scale(v):
    return lax.rsqrt(jnp.mean(v * v, axis=-1, keepdims=True) + EPS)


def _proj_pair(xg, win_ref, pair):
    out = _dot(xg, win_ref[:, 2 * pair * SLAB:(2 * pair + 2) * SLAB])
    return out[:, :SLAB], out[:, SLAB:]


def _even_mix(xg, r, mix_ref, t, anchor, win_ref, convw_ref, poolw_ref,
              pscale_ref, hc_carry, xp_carry):
    tm = xg.shape[0]
    first = t == 0
    groups_per_slab = SLAB // GROUP_DIM
    r2 = r * r
    rh = 0.5 * r
    pos1 = t * tm + lax.broadcasted_iota(jnp.int32, (tm, GROUP_DIM), 0) + 1
    inv_pos = 1.0 / pos1.astype(_F32)

    del poolw_ref

    def slab_b(j, pair):
        lanes = slice(j * SLAB, (j + 1) * SLAB)
        xp, zp = _proj_pair(xg, win_ref, pair)
        if pair == 0:
            xp = anchor(xp)
        xp = xp * r
        carry = jnp.where(first, 0.0, xp_carry[:, lanes])
        xp_carry[:, lanes] = xp[tm - POOL_HALO:, :]
        acc = jnp.concatenate([carry, xp], axis=0)
        shift = 1
        for _ in range(j * groups_per_slab):
            acc = acc + pltpu.roll(acc, shift, axis=0)
            shift *= 2
        pooled = []
        for k in range(groups_per_slab):
            w = POOL_WINDOWS[j * groups_per_slab + k]
            acc = acc + pltpu.roll(acc, shift, axis=0)
            shift *= 2
            inv_count = jnp.where(pos1 < w, inv_pos, 1.0 / w)
            pooled.append(acc[POOL_HALO:, :GROUP_DIM] * inv_count
                          - xp[:, k * GROUP_DIM:(k + 1) * GROUP_DIM])
            acc = acc[:, GROUP_DIM:]
        mixed = jnp.concatenate(pooled, axis=-1)
        return mixed * pscale_ref[:, lanes] * _silu_of_double(zp * rh)

    def slab_a(j, pair):
        lanes = slice(j * SLAB, (j + 1) * SLAB)
        xa, gc = _proj_pair(xg, win_ref, pair)
        if pair == 0:
            xa = anchor(xa)
        cw = convw_ref[:, lanes]
        hc = gc * xa * r2
        carry = jnp.where(first, 0.0, hc_carry[:, lanes])
        hc_carry[:, lanes] = hc[tm - CONV_HALO:, :]
        ext = jnp.concatenate([carry, hc], axis=0)
        hc1 = pltpu.roll(ext, 1, axis=0)[CONV_HALO:, :]
        hc2 = pltpu.roll(ext, 2, axis=0)[CONV_HALO:, :]
        conv = cw[0:1] * hc2 + cw[1:2] * hc1 + cw[2:3] * hc
        gb, za = _proj_pair(xg, win_ref, pair + 1)
        return gb * r * conv * _silu_of_double(za * rh)

    pair = 0
    for pos, (kind, j) in enumerate(EVEN_ORDER):
        if kind == "a":
            val = slab_a(j, pair)
            pair += 2
        else:
            val = slab_b(j, pair)
            pair += 1
        mix_ref[:, pos * SLAB:(pos + 1) * SLAB] = val.astype(_BF16)


def _odd_mix(xg, r, mix_ref, t, anchor, win_ref, lng_ref, lnb_ref, ws_ref,
             bias_ref):
    del t
    tm = xg.shape[0]
    n_chunks = tm // CHUNK
    n_slabs = D_MODEL // SLAB
    heads_per_slab = SLAB // C_HEAD_DIM

    pv = [p for pair in range(n_slabs // 2)
          for p in _proj_pair(xg, win_ref, pair)]
    pv[0] = anchor(pv[0])
    uz = [_proj_pair(xg, win_ref, n_slabs // 2 + j) for j in range(UZ_EARLY)]
    mu = jnp.sum((pv[0] + pv[1]) + (pv[2] + pv[3]), axis=-1,
                 keepdims=True) * (1.0 / D_MODEL)
    sq = [p * p for p in pv]
    ms = jnp.sum((sq[0] + sq[1]) + (sq[2] + sq[3]), axis=-1,
                 keepdims=True) * (1.0 / D_MODEL)
    var = ms - mu * mu
    vc = [p - mu for p in pv]
    rstd = r * lax.rsqrt(r * r * var + EPS)
    rh = 0.5 * r

    row = lax.broadcasted_iota(jnp.int32, (CHUNK, CHUNK), 0)
    col = lax.broadcasted_iota(jnp.int32, (CHUNK, CHUNK), 1)
    svs = []
    for j in range(n_slabs):
        lanes = slice(j * SLAB, (j + 1) * SLAB)
        vn = (vc[j] * rstd * lng_ref[:, lanes] + lnb_ref[:, lanes]).astype(_BF16)
        sv = []
        for k in range(heads_per_slab):
            hd = j * heads_per_slab + k
            ws = jnp.where(col <= row, ws_ref[hd], 0.0).astype(_BF16)
            hl = slice(k * C_HEAD_DIM, (k + 1) * C_HEAD_DIM)
            rhs = jnp.concatenate(
                [vn[n * CHUNK:(n + 1) * CHUNK, hl] for n in range(n_chunks)],
                axis=-1)
            out = _dot(ws, rhs)
            sv.append(jnp.concatenate(
                [out[:, n * C_HEAD_DIM:(n + 1) * C_HEAD_DIM]
                 for n in range(n_chunks)], axis=0))
        bias = bias_ref[:, lanes]
        svs.append(jnp.concatenate(sv, axis=-1)
                   + jnp.concatenate([bias] * n_chunks, axis=0))
    uz += [_proj_pair(xg, win_ref, n_slabs // 2 + j)
           for j in range(UZ_EARLY, n_slabs)]
    for j in range(n_slabs):
        u, z = uz[j]
        mix_ref[:, j * SLAB:(j + 1) * SLAB] = (
            u * r * svs[j] * _silu_of_double(z * rh)).astype(_BF16)


def _stage_weights(w_hbm, stage, sem, store_chunk):
    for c in range(w_hbm.shape[0] // STAGE_ROWS):
        copy = pltpu.make_async_copy(
            w_hbm.at[pl.ds(c * STAGE_ROWS, STAGE_ROWS), :], stage, sem)
        copy.start()
        copy.wait()
        store_chunk(c * STAGE_ROWS, stage)


def _layer_kernel(layer, idx, mix_fn, fold_fn, view_fn, slab_cols, mix_rows,
                  n_tiles, tiles_per_seq, n_params, n_extra, *refs):
    x_ref, xlag_ref, pre_ref, post_ref = refs[:4]
    raw_params = refs[4:4 + n_params]
    (win_hbm, wout_hbm, o_ref, mix_scr, win_scr, wout_scr, stage_in, stage_out,
     sem) = refs[4 + n_params:13 + n_params]
    extra = refs[13 + n_params:13 + n_params + n_extra]
    carries = refs[13 + n_params + n_extra:]
    s = pl.program_id(0)
    params = view_fn(idx, raw_params, extra)

    def mixers(anchor):
        x = x_ref[0]
        xg = (x * pre_ref[layer:layer + 1, :]).astype(_BF16)
        mix_fn(xg, _row_scale(x), mix_scr, s % tiles_per_seq, anchor, win_scr,
               *params, *carries)

    def finish():
        m = _dot(mix_scr[...], wout_scr[...])
        o_ref[0] = (xlag_ref[0]
                    + m * _row_scale(m) * post_ref[layer:layer + 1, :])

    def after_finish(val):
        never = s < 0
        d = o_ref.shape[2]
        for c in range(0, d, SLAB):
            val = jnp.where(never, o_ref[0, :, c:c + SLAB], val)
        return val

    @pl.when(s == 0)
    def _():
        view_fn(idx, raw_params, extra, init=True)

        def store_in(row0, stage):
            for pos, col in enumerate(slab_cols):
                win_scr[row0:row0 + STAGE_ROWS, pos * SLAB:(pos + 1) * SLAB] = (
                    fold_fn(col, stage[:, col:col + SLAB], *params).astype(_BF16))

        def store_out(row0, stage):
            dst = mix_rows.index(row0) * SLAB
            wout_scr[dst:dst + STAGE_ROWS, :] = stage[...].astype(_BF16)

        _stage_weights(win_hbm.at[idx], stage_in, sem, store_in)
        _stage_weights(wout_hbm.at[idx], stage_out, sem, store_out)
        mixers(lambda val: val)

    @pl.when(jnp.logical_and(s > 0, s < n_tiles))
    def _():
        finish()
        mixers(after_finish)

    @pl.when(s == n_tiles)
    def _():
        finish()


def _full(shape):
    return pl.BlockSpec(shape, lambda s: (0,) * len(shape),
                        pipeline_mode=pl.Buffered(1))


def _layer_call(name, layer, idx, mix_fn, fold_fn, view_fn, slab_cols, mix_rows,
                x, pre, post, w_in, params, w_out, extra_shapes, carry_shapes):
    bsz, seq, d = x.shape
    tiles_per_seq = seq // ROW_TILE
    n_tiles = bsz * tiles_per_seq
    assert STAGE_ROWS == SLAB
    assert sorted(slab_cols) == list(range(0, w_in.shape[2], SLAB))
    assert sorted(mix_rows) == list(range(0, w_out.shape[1], SLAB))

    def tile_map(lag):
        def index_map(s):
            tile = jnp.clip(s - lag, 0, n_tiles - 1)
            return (tile // tiles_per_seq, tile % tiles_per_seq, 0)
        return index_map

    row_block = (1, ROW_TILE, d)
    small = (pre, post, *params)
    scratch = [pltpu.VMEM((ROW_TILE, d), _BF16),
               pltpu.VMEM(w_in.shape[1:], _BF16),
               pltpu.VMEM(w_out.shape[1:], _BF16),
               pltpu.VMEM((STAGE_ROWS, w_in.shape[2]), _F32),
               pltpu.VMEM((STAGE_ROWS, w_out.shape[2]), _F32),
               pltpu.SemaphoreType.DMA]
    scratch += [pltpu.VMEM(shape, _F32) for shape in extra_shapes + carry_shapes]
    return pl.pallas_call(
        functools.partial(_layer_kernel, layer, idx, mix_fn, fold_fn, view_fn,
                          slab_cols, mix_rows, n_tiles, tiles_per_seq,
                          len(params), len(extra_shapes)),
        name=name,
        grid=(n_tiles + 1,),
        in_specs=([pl.BlockSpec(row_block, tile_map(0)),
                   pl.BlockSpec(row_block, tile_map(1))]
                  + [_full(p.shape) for p in small]
                  + [pl.BlockSpec(memory_space=pl.ANY)] * 2),
        out_specs=pl.BlockSpec(row_block, tile_map(1)),
        out_shape=jax.ShapeDtypeStruct(x.shape, x.dtype),
        scratch_shapes=scratch,
        compiler_params=pltpu.CompilerParams(
            dimension_semantics=("arbitrary",),
            vmem_limit_bytes=VMEM_LIMIT_BYTES),
    )(x, x, *small, w_in, w_out)


def _even_fold(col, w, convw_ref, poolw_ref, pscale_ref):
    del convw_ref, pscale_ref
    j, rem = divmod(col - 4 * A_WIDTH, SLAB)
    if 0 <= j < B_WIDTH // SLAB and rem == 0:
        groups_per_slab = SLAB // GROUP_DIM
        return jnp.concatenate([
            jnp.dot(w[:, k * GROUP_DIM:(k + 1) * GROUP_DIM],
                    poolw_ref[j * groups_per_slab + k],
                    preferred_element_type=_F32,
                    precision=lax.Precision.HIGHEST)
            for k in range(groups_per_slab)], axis=-1)
    return w


def _no_fold(col, w, *params):
    del col, params
    return w


def _even_views(idx, raw, extra, init=False):
    del extra, init
    convw_ref, poolw_ref, pscale_ref = raw
    return (convw_ref.at[idx], poolw_ref.at[idx], pscale_ref.at[idx:idx + 1])


def _odd_views(idx, raw, extra, init=False):
    lng_ref, lnb_ref, ws_ref, bs_ref = raw
    (bias_scr,) = extra
    if init:
        n_cols = C_HEADS * C_HEAD_DIM
        head_of_col = lax.broadcasted_iota(
            jnp.int32, (C_HEADS, n_cols), 1) // C_HEAD_DIM
        select = (head_of_col == lax.broadcasted_iota(
            jnp.int32, (C_HEADS, n_cols), 0)).astype(_F32)
        bias_scr[...] = lax.dot_general(
            bs_ref[idx], select, (((0,), (0,)), ((), ())),
            preferred_element_type=_F32, precision=lax.Precision.HIGHEST)
    return (lng_ref.at[idx:idx + 1], lnb_ref.at[idx:idx + 1], ws_ref.at[idx],
            bias_scr)


def kernel(x, pre_norm, post_norm, even_w_in, even_conv_w, even_pool_w, even_pool_scale, even_w_out, odd_w_in, odd_ln_g, odd_ln_b, odd_w_s, odd_b_s, odd_w_out):
    depth = pre_norm.shape[0]
    for i in range(depth):
        j = i // 2
        if i % 2 == 0:
            params = (even_conv_w, even_pool_w, even_pool_scale)
            carries = [(CONV_HALO, A_WIDTH), (POOL_HALO, B_WIDTH)]
            x = _layer_call("even_layer", i, j, _even_mix, _even_fold,
                            _even_views, EVEN_SLAB_COLS, EVEN_MIX_ROWS, x,
                            pre_norm, post_norm, even_w_in, params, even_w_out,
                            [], carries)
        else:
            params = (odd_ln_g, odd_ln_b, odd_w_s, odd_b_s)
            x = _layer_call("odd_layer", i, j, _odd_mix, _no_fold, _odd_views,
                            ODD_SLAB_COLS, ODD_MIX_ROWS, x, pre_norm, post_norm,
                            odd_w_in, params, odd_w_out,
                            [(CHUNK, C_HEADS * C_HEAD_DIM)], [])
    return x
```
